```python
import math
import jax, jax.numpy as jnp
from jax import lax
import numpy as np

D_MODEL = 2048
BATCH = 2
SEQ = 8192
DEPTH = 2
DEC_BATCH = 16
DEC_SEQ = 64
PAST_LEN = 4096

CHUNK = 64
GROUP_WIDTH = D_MODEL // 2
MIX_WIDTH = 2 * GROUP_WIDTH
LRU_HEADS = 8
LRU_HEAD_DIM = GROUP_WIDTH // LRU_HEADS
LRU_C = 8.0
CONV_WIDTH = 4
MLP_CHUNK = 128
MLP_HEADS = 8
MLP_HEAD_DIM = GROUP_WIDTH // MLP_HEADS
D_FF = ((8 * D_MODEL // 3 + 255) // 256) * 256
EPS = 1e-6

kernel_name = "hybrid_rglru_chunkmlp_stream_step"


def rmsnorm(x, g):
    xf = x.astype(jnp.float32)
    y = xf * lax.rsqrt(jnp.mean(xf * xf, axis=-1, keepdims=True) + EPS)
    return (y * g.astype(jnp.float32)).astype(x.dtype)


def layernorm(x, g, b):
    xf = x.astype(jnp.float32)
    mu = jnp.mean(xf, axis=-1, keepdims=True)
    var = jnp.mean(jnp.square(xf - mu), axis=-1, keepdims=True)
    y = (xf - mu) * lax.rsqrt(var + EPS)
    return (y * g.astype(jnp.float32) + b.astype(jnp.float32)).astype(x.dtype)


def causal_conv(x, buf, w, b):
    T = x.shape[1]
    xp = jnp.concatenate([buf.astype(x.dtype), x], axis=1)
    y = sum(xp[:, k:k + T] * w[k] for k in range(CONV_WIDTH)) + b
    return y, xp[:, -(CONV_WIDTH - 1):]


def block_diag_linear(x, w, b):
    B, T, _ = x.shape
    xh = x.reshape(B, T, LRU_HEADS, LRU_HEAD_DIM)
    y = jnp.einsum('bthi,hij->bthj', xh, w).reshape(B, T, GROUP_WIDTH)
    return y + b


def rglru(x, r, i, lam, h0):
    xf, rf, if_ = (t.astype(jnp.float32) for t in (x, r, i))
    log_a = -LRU_C * rf * jax.nn.softplus(-lam.astype(jnp.float32))
    a = jnp.exp(log_a)
    mult = jnp.sqrt(-jnp.expm1(2.0 * log_a))
    bterm = mult * (if_ * xf)
    bterm = bterm.at[:, 0].add(a[:, 0] * h0.astype(jnp.float32))

    def combine(left, right):
        a1, b1 = left
        a2, b2 = right
        return a1 * a2, a2 * b1 + b2

    _, h = lax.associative_scan(combine, (a, bterm), axis=1)
    return h.astype(x.dtype), h[:, -1].astype(x.dtype)


def chunk_token_mlp(u, v, w_s, b_s):
    B, T, C = v.shape
    n = -(-T // MLP_CHUNK)
    pad = n * MLP_CHUNK - T
    vp = jnp.pad(v, ((0, 0), (0, pad), (0, 0))).reshape(B, n, MLP_CHUNK, MLP_HEADS, MLP_HEAD_DIM)
    mask = jnp.tril(jnp.ones((MLP_CHUNK, MLP_CHUNK), dtype=bool))
    w = jnp.where(mask[None], w_s, jnp.zeros_like(w_s))
    mixed = jnp.einsum('hts,bnshd->bnthd', w, vp) + b_s.T[None, None, :, :, None]
    mixed = mixed.reshape(B, n * MLP_CHUNK, C)[:, :T]
    return u * mixed


def layer(x, conv_buf, h0, norm1, w_in, conv_w, conv_b, w_rgate, b_rgate, w_igate, b_igate,
          lru_param, v_ln_g, v_ln_b, w_spatial, b_spatial, gn_a, gn_b, w_out,
          norm2, w_gate, w_up, w_down):
    xn = rmsnorm(x, norm1)
    proj = xn @ w_in
    xa, ga, u, v = jnp.split(proj, 4, axis=-1)
    xc, new_buf = causal_conv(xa, conv_buf, conv_w, conv_b)
    r = jax.nn.sigmoid(block_diag_linear(xc, w_rgate, b_rgate))
    ig = jax.nn.sigmoid(block_diag_linear(xc, w_igate, b_igate))
    y_lru, h_last = rglru(xc, r, ig, lru_param, h0)
    out_a = y_lru * jax.nn.gelu(ga)
    u_act = jax.nn.gelu(u)
    v_n = layernorm(jax.nn.gelu(v), v_ln_g, v_ln_b)
    out_b = chunk_token_mlp(u_act, v_n, w_spatial, b_spatial)
    mixed = jnp.concatenate([rmsnorm(out_a, gn_a), rmsnorm(out_b, gn_b)], axis=-1)
    h = x + mixed @ w_out
    hn = rmsnorm(h, norm2)
    y = h + (jax.nn.silu(hn @ w_gate) * (hn @ w_up)) @ w_down
    return y, new_buf, h_last, v_n


def setup_inputs(seed: int = 0) -> dict:
    key = jax.random.key(seed)
    ks = jax.random.split(key, 32)
    f32 = jnp.float32
    nrm = lambda k, s, sc: jax.random.normal(k, s, f32) * sc
    u0 = jax.random.uniform(ks[10], (DEPTH, GROUP_WIDTH), f32, 0.9, 0.999)
    a0 = u0 ** (1.0 / LRU_C)
    lru_param = jnp.log(a0) - jnp.log1p(-a0)
    return {
        "x_prompt": nrm(ks[0], (BATCH, SEQ, D_MODEL), 1.0),
        "x_sample": nrm(ks[1], (DEC_BATCH, DEC_SEQ, D_MODEL), 1.0),
        "state_conv": nrm(ks[2], (DEPTH, DEC_BATCH, CONV_WIDTH - 1, GROUP_WIDTH), 0.5),
        "state_lru": nrm(ks[3], (DEPTH, DEC_BATCH, GROUP_WIDTH), 0.5),
        "norm1": 1.0 + nrm(ks[4], (DEPTH, D_MODEL), 0.02),
        "w_in": nrm(ks[5], (DEPTH, D_MODEL, 4 * GROUP_WIDTH), D_MODEL ** -0.5),
        "conv_w": nrm(ks[6], (DEPTH, CONV_WIDTH, GROUP_WIDTH), CONV_WIDTH ** -0.5),
        "conv_b": nrm(ks[7], (DEPTH, GROUP_WIDTH), 0.01),
        "w_rgate": nrm(ks[8], (DEPTH, LRU_HEADS, LRU_HEAD_DIM, LRU_HEAD_DIM), LRU_HEAD_DIM ** -0.5),
        "b_rgate": nrm(ks[9], (DEPTH, GROUP_WIDTH), 0.01),
        "w_igate": nrm(ks[11], (DEPTH, LRU_HEADS, LRU_HEAD_DIM, LRU_HEAD_DIM), LRU_HEAD_DIM ** -0.5),
        "b_igate": nrm(ks[12], (DEPTH, GROUP_WIDTH), 0.01),
        "lru_param": lru_param,
        "v_ln_g": 1.0 + nrm(ks[13], (DEPTH, GROUP_WIDTH), 0.02),
        "v_ln_b": nrm(ks[14], (DEPTH, GROUP_WIDTH), 0.01),
        "w_spatial": nrm(ks[15], (DEPTH, MLP_HEADS, MLP_CHUNK, MLP_CHUNK), MLP_CHUNK ** -0.5),
        "b_spatial": 1.0 + nrm(ks[16], (DEPTH, MLP_HEADS, MLP_CHUNK), 0.1),
        "gn_a": 1.0 + nrm(ks[17], (DEPTH, GROUP_WIDTH), 0.02),
        "gn_b": 1.0 + nrm(ks[18], (DEPTH, GROUP_WIDTH), 0.02),
        "w_out": nrm(ks[19], (DEPTH, MIX_WIDTH, D_MODEL), MIX_WIDTH ** -0.5),
        "norm2": 1.0 + nrm(ks[20], (DEPTH, D_MODEL), 0.02),
        "w_gate": nrm(ks[21], (DEPTH, D_MODEL, D_FF), D_MODEL ** -0.5),
        "w_up": nrm(ks[22], (DEPTH, D_MODEL, D_FF), D_MODEL ** -0.5),
        "w_down": nrm(ks[23], (DEPTH, D_FF, D_MODEL), D_FF ** -0.5),
        "norm_f": 1.0 + nrm(ks[24], (D_MODEL,), 0.02),
    }


def reference(x_prompt, x_sample, state_conv, state_lru, norm1, w_in, conv_w, conv_b,
              w_rgate, b_rgate, w_igate, b_igate, lru_param, v_ln_g, v_ln_b,
              w_spatial, b_spatial, gn_a, gn_b, w_out, norm2, w_gate, w_up, w_down, norm_f):
    xp = x_prompt
    xs = x_sample
    buf_p = jnp.zeros((x_prompt.shape[0], CONV_WIDTH - 1, GROUP_WIDTH), x_prompt.dtype)
    h_p = jnp.zeros((x_prompt.shape[0], GROUP_WIDTH), x_prompt.dtype)
    conv_p, lru_p, conv_s, lru_s, vrows_s = [], [], [], [], []
    for l in range(DEPTH):
        params = (norm1[l], w_in[l], conv_w[l], conv_b[l], w_rgate[l], b_rgate[l],
                  w_igate[l], b_igate[l], lru_param[l], v_ln_g[l], v_ln_b[l],
                  w_spatial[l], b_spatial[l], gn_a[l], gn_b[l], w_out[l],
                  norm2[l], w_gate[l], w_up[l], w_down[l])
        xp, nb_p, nh_p, _ = layer(xp, buf_p, h_p, *params)
        xs, nb_s, nh_s, vn_s = layer(xs, state_conv[l], state_lru[l], *params)
        conv_p.append(nb_p)
        lru_p.append(nh_p)
        conv_s.append(nb_s)
        lru_s.append(nh_s)
        vrows_s.append(vn_s)
    y_prompt = rmsnorm(xp, norm_f)
    y_sample = rmsnorm(xs, norm_f)
    new_conv_prompt = jnp.stack(conv_p)
    new_lru_prompt = jnp.stack(lru_p)
    new_conv_sample = jnp.stack(conv_s)
    new_lru_sample = jnp.stack(lru_s)
    new_vrows_sample = jnp.stack(vrows_s)
    return (y_prompt, y_sample, new_conv_prompt, new_lru_prompt, new_conv_sample, new_lru_sample, new_vrows_sample)
```

```python
import functools
import math

import jax
import jax.numpy as jnp
from jax import lax
from jax.experimental import pallas as pl
from jax.experimental.pallas import tpu as pltpu

D_MODEL = 2048
GROUP_WIDTH = D_MODEL // 2
HEADS = 8
HEAD_DIM = GROUP_WIDTH // HEADS
CONV_WIDTH = 4
MLP_CHUNK = 128
LRU_C = 8.0
EPS = 1e-6

SUBLANES = 8
CARRY_ROWS = SUBLANES

MIXER_ROWS = 256
FFN_ROWS = 512
FFN_COLS = 512
VMEM_LIMIT = 56 * 1024 * 1024

_CW0, _CB, _BR, _BI, _LAM, _LNG, _LNB, _GNA, _GNB = 0, 4, 5, 6, 7, 8, 9, 10, 11
_CVEC_ROWS = 16

_BF16 = jnp.bfloat16
_F32 = jnp.float32


def _dot(a, b):
    return jnp.dot(a, b, preferred_element_type=_F32)


def _rms(x, g):
    return x * lax.rsqrt(jnp.mean(x * x, axis=-1, keepdims=True) + EPS) * g


def _gelu(x):
    c = math.sqrt(2.0 / math.pi)
    return x * (0.5 * (1.0 + jnp.tanh(c * (x + 0.044715 * (x * x * x)))))


def _sigmoid(x):
    return 1.0 / (1.0 + jnp.exp(-x))


def _mixer_kernel(x_ref, conv0_ref, h0_ref, norm1_ref, cvec_ref, win_ref, wgate_ref, wsp_ref,
                  bsb_ref, wout_ref, *rest, seg, nseg, carry, emit_vrows):
    if emit_vrows:
        y_ref, convo_ref, ho_ref, vrows_ref, xpad, a3, b3, h3, hcar = rest
    else:
        y_ref, convo_ref, ho_ref, xpad, a3, b3, h3, hcar = rest
        vrows_ref = None
    rows = seg * nseg
    gw = GROUP_WIDTH

    def vec(k):
        return cvec_ref[k:k + 1, :]

    if carry:
        t = pl.program_id(1)

        @pl.when(t == 0)
        def _():
            xpad[0, 0:CARRY_ROWS, :] = conv0_ref[0]
            hcar[...] = h0_ref[0]

        @pl.when(t > 0)
        def _():
            xpad[0, 0:CARRY_ROWS, :] = xpad[0, seg:seg + CARRY_ROWS, :]
    else:
        for s in range(nseg):
            xpad[s, 0:CARRY_ROWS, :] = conv0_ref[s]

    x = x_ref[...]
    xnb = _rms(x, norm1_ref[...]).astype(_BF16)

    xa = _dot(xnb, win_ref[:, 0:gw])
    for s in range(nseg):
        xpad[s, CARRY_ROWS:CARRY_ROWS + seg, :] = xa[s * seg:(s + 1) * seg]
        convo_ref[s] = xa[(s + 1) * seg - CARRY_ROWS:(s + 1) * seg]

    def shifted(k):
        parts = [xpad[s, CARRY_ROWS - k:CARRY_ROWS - k + seg, :] for s in range(nseg)]
        return parts[0] if nseg == 1 else jnp.concatenate(parts, axis=0)

    xc = (shifted(3) * vec(_CW0) + shifted(2) * vec(_CW0 + 1) + shifted(1) * vec(_CW0 + 2)
          + xa * vec(_CW0 + 3) + vec(_CB))

    xcb = xc.astype(_BF16)
    gr, gi = [], []
    for h in range(HEADS):
        g = _dot(xcb[:, h * HEAD_DIM:(h + 1) * HEAD_DIM], wgate_ref[h])
        gr.append(g[:, :HEAD_DIM])
        gi.append(g[:, HEAD_DIM:])
    r = _sigmoid(jnp.concatenate(gr, axis=1) + vec(_BR))
    ig = _sigmoid(jnp.concatenate(gi, axis=1) + vec(_BI))

    neg_lam = -vec(_LAM)
    softplus = jnp.maximum(neg_lam, 0.0) + jnp.log1p(jnp.exp(-jnp.abs(neg_lam)))
    log_a = (-LRU_C * r) * softplus
    a = jnp.exp(log_a)
    bterm = jnp.sqrt(-jnp.tanh(log_a) * (a * a + 1.0)) * (ig * xc)

    for c in range(HEADS):
        cols = slice(c * HEAD_DIM, (c + 1) * HEAD_DIM)
        a3[:, c * SUBLANES:(c + 1) * SUBLANES, :] = a[:, cols].reshape(rows // SUBLANES, SUBLANES, HEAD_DIM)
        b3[:, c * SUBLANES:(c + 1) * SUBLANES, :] = bterm[:, cols].reshape(rows // SUBLANES, SUBLANES, HEAD_DIM)

    groups_per_seg = seg // SUBLANES

    def scan_body(j, hs):
        new = []
        for s in range(nseg):
            h = hs[s]
            jj = s * groups_per_seg + j
            for rr in range(SUBLANES):
                at = a3[jj, pl.ds(rr, HEADS, stride=SUBLANES), :]
                bt = b3[jj, pl.ds(rr, HEADS, stride=SUBLANES), :]
                h = at * h + bt
                h3[jj, pl.ds(rr, HEADS, stride=SUBLANES), :] = h
            new.append(h)
        return tuple(new)

    h_init = tuple(hcar[...] if carry else h0_ref[s] for s in range(nseg))
    hs = lax.fori_loop(0, groups_per_seg, scan_body, h_init)
    if carry:
        hcar[...] = hs[0]
    for s in range(nseg):
        ho_ref[s] = hs[s]

    y_lru = jnp.concatenate(
        [h3[:, c * SUBLANES:(c + 1) * SUBLANES, :].reshape(rows, HEAD_DIM) for c in range(HEADS)], axis=1)
    ga = _dot(xnb, win_ref[:, gw:2 * gw])
    na = _rms(y_lru * _gelu(ga), vec(_GNA)).astype(_BF16)

    u = _dot(xnb, win_ref[:, 2 * gw:3 * gw])
    v = _dot(xnb, win_ref[:, 3 * gw:4 * gw])
    vg = _gelu(v)
    mu = jnp.mean(vg, axis=-1, keepdims=True)
    vc = vg - mu
    var = jnp.mean(vc * vc, axis=-1, keepdims=True)
    v_n = vc * lax.rsqrt(var + EPS) * vec(_LNG) + vec(_LNB)
    if emit_vrows:
        vrows_ref[...] = v_n
    vb = v_n.astype(_BF16)

    chunk = min(seg, MLP_CHUNK)
    nchunk = rows // chunk
    tri = (lax.broadcasted_iota(jnp.int32, (MLP_CHUNK, MLP_CHUNK), 0)
           >= lax.broadcasted_iota(jnp.int32, (MLP_CHUNK, MLP_CHUNK), 1))
    mixed_cols = []
    for h in range(HEADS):
        cols = slice(h * HEAD_DIM, (h + 1) * HEAD_DIM)
        w = jnp.where(tri, wsp_ref[h], 0.0).astype(_BF16)[:chunk, :chunk]
        vh = jnp.concatenate([vb[k * chunk:(k + 1) * chunk, cols] for k in range(nchunk)], axis=1)
        m = _dot(w, vh)
        bias = bsb_ref[h][:chunk]
        mixed_cols.append(jnp.concatenate(
            [m[:, k * HEAD_DIM:(k + 1) * HEAD_DIM] + bias for k in range(nchunk)], axis=0))
    mixed = jnp.concatenate(mixed_cols, axis=1)
    nb = _rms(_gelu(u) * mixed, vec(_GNB)).astype(_BF16)

    y_ref[...] = x + _dot(jnp.concatenate([na, nb], axis=1), wout_ref[...])


def _const_spec(shape):
    zeros = (0,) * len(shape)
    return pl.BlockSpec(shape, lambda *_: zeros, pipeline_mode=pl.Buffered(1))


def _mixer(x2d, conv0, h0, norm1, cvec, win, wgate, wsp, bsb, wout, *, seg, nseg, carry,
           emit_vrows, name):
    rows = seg * nseg
    total = x2d.shape[0]
    nseq = conv0.shape[0]
    gw = GROUP_WIDTH
    if carry:
        steps = total // nseq // rows
        grid = (nseq, steps)
        row_map = lambda b, t: (b * steps + t, 0)
        seq_map = lambda b, t: (b, 0, 0)
    else:
        grid = (total // rows,)
        row_map = lambda i: (i, 0)
        seq_map = lambda i: (i, 0, 0)
    out_shape = [jax.ShapeDtypeStruct((total, D_MODEL), _F32),
                 jax.ShapeDtypeStruct((nseq, CARRY_ROWS, gw), _F32),
                 jax.ShapeDtypeStruct((nseq, HEADS, HEAD_DIM), _F32)]
    out_specs = [pl.BlockSpec((rows, D_MODEL), row_map),
                 pl.BlockSpec((nseg, CARRY_ROWS, gw), seq_map),
                 pl.BlockSpec((nseg, HEADS, HEAD_DIM), seq_map)]
    if emit_vrows:
        out_shape.append(jax.ShapeDtypeStruct((total, gw), _F32))
        out_specs.append(pl.BlockSpec((rows, gw), row_map))
    scan_shape = (rows // SUBLANES, HEADS * SUBLANES, HEAD_DIM)
    return pl.pallas_call(
        functools.partial(_mixer_kernel, seg=seg, nseg=nseg, carry=carry, emit_vrows=emit_vrows),
        grid=grid,
        in_specs=[pl.BlockSpec((rows, D_MODEL), row_map),
                  pl.BlockSpec((nseg, CARRY_ROWS, gw), seq_map),
                  pl.BlockSpec((nseg, HEADS, HEAD_DIM), seq_map),
                  _const_spec(norm1.shape), _const_spec(cvec.shape), _const_spec(win.shape),
                  _const_spec(wgate.shape), _const_spec(wsp.shape), _const_spec(bsb.shape),
                  _const_spec(wout.shape)],
        out_specs=out_specs,
        out_shape=out_shape,
        scratch_shapes=[pltpu.VMEM((nseg, seg + CARRY_ROWS, gw), _F32),
                        pltpu.VMEM(scan_shape, _F32), pltpu.VMEM(scan_shape, _F32),
                        pltpu.VMEM(scan_shape, _F32), pltpu.VMEM((HEADS, HEAD_DIM), _F32)],
        compiler_params=pltpu.CompilerParams(
            dimension_semantics=("arbitrary",) * len(grid), vmem_limit_bytes=VMEM_LIMIT),
        name=name,
    )(x2d, conv0, h0, norm1, cvec, win, wgate, wsp, bsb, wout)


def _ffn_kernel(h_ref, norm2_ref, wg_ref, wu_ref, wd_ref, normf_ref, o_ref, hn_ref, acc_ref, *, final):
    j = pl.program_id(1)

    @pl.when(j == 0)
    def _():
        hn_ref[...] = _rms(h_ref[...], norm2_ref[...]).astype(_BF16)
        acc_ref[...] = jnp.zeros_like(acc_ref)

    hn = hn_ref[...]
    g = _dot(hn, wg_ref[...])
    u = _dot(hn, wu_ref[...])
    act = ((g * _sigmoid(g)) * u).astype(_BF16)
    acc_ref[...] += _dot(act, wd_ref[...])

    @pl.when(j == pl.num_programs(1) - 1)
    def _():
        y = h_ref[...] + acc_ref[...]
        if final:
            y = _rms(y, normf_ref[...])
        o_ref[...] = y


def _ffn(h2d, norm2, wg, wu, wd, normf, *, final, name):
    total = h2d.shape[0]
    d_ff = wg.shape[1]
    rows = min(FFN_ROWS, total)
    grid = (total // rows, d_ff // FFN_COLS)
    return pl.pallas_call(
        functools.partial(_ffn_kernel, final=final),
        grid=grid,
        in_specs=[pl.BlockSpec((rows, D_MODEL), lambda i, j: (i, 0)),
                  _const_spec(norm2.shape),
                  pl.BlockSpec((D_MODEL, FFN_COLS), lambda i, j: (0, j)),
                  pl.BlockSpec((D_MODEL, FFN_COLS), lambda i, j: (0, j)),
                  pl.BlockSpec((FFN_COLS, D_MODEL), lambda i, j: (j, 0)),
                  _const_spec(normf.shape)],
        out_specs=pl.BlockSpec((rows, D_MODEL), lambda i, j: (i, 0)),
        out_shape=jax.ShapeDtypeStruct((total, D_MODEL), _F32),
        scratch_shapes=[pltpu.VMEM((rows, D_MODEL), _BF16), pltpu.VMEM((rows, D_MODEL), _F32)],
        compiler_params=pltpu.CompilerParams(
            dimension_semantics=("arbitrary", "arbitrary"), vmem_limit_bytes=VMEM_LIMIT),
        name=name,
    )(h2d, norm2, wg, wu, wd, normf)


def _pad_conv_state(buf):
    return jnp.pad(buf, ((0, 0), (CARRY_ROWS - (CONV_WIDTH - 1), 0), (0, 0)))


def kernel(x_prompt, x_sample, state_conv, state_lru, norm1, w_in, conv_w, conv_b, w_rgate, b_rgate,
           w_igate, b_igate, lru_param, v_ln_g, v_ln_b, w_spatial, b_spatial, gn_a, gn_b, w_out,
           norm2, w_gate, w_up, w_down, norm_f):
    depth = w_in.shape[0]
    batch, seq, _ = x_prompt.shape
    dec_batch, dec_seq, _ = x_sample.shape
    gw = GROUP_WIDTH

    xp = x_prompt.reshape(batch * seq, D_MODEL)
    xs = x_sample.reshape(dec_batch * dec_seq, D_MODEL)
    conv0_p = jnp.zeros((batch, CARRY_ROWS, gw), _F32)
    h0_p = jnp.zeros((batch, HEADS, HEAD_DIM), _F32)
    normf = norm_f.reshape(1, D_MODEL)

    conv_p, lru_p, conv_s, lru_s, vrows_s = [], [], [], [], []
    for l in range(depth):
        cvec = jnp.concatenate(
            [conv_w[l], conv_b[l][None], b_rgate[l][None], b_igate[l][None], lru_param[l][None],
             v_ln_g[l][None], v_ln_b[l][None], gn_a[l][None], gn_b[l][None],
             jnp.zeros((_CVEC_ROWS - 12, gw), _F32)], axis=0)
        wgate = jnp.concatenate([w_rgate[l], w_igate[l]], axis=-1).astype(_BF16)
        bsb = jnp.broadcast_to(b_spatial[l][:, :, None], (HEADS, MLP_CHUNK, HEAD_DIM))
        mixer_w = (norm1[l].reshape(1, D_MODEL), cvec, w_in[l].astype(_BF16), wgate, w_spatial[l], bsb,
                   w_out[l].astype(_BF16))
        ffn_w = (norm2[l].reshape(1, D_MODEL), w_gate[l].astype(_BF16), w_up[l].astype(_BF16),
                 w_down[l].astype(_BF16), normf)
        final = l == depth - 1

        hp, cp, lp = _mixer(xp, conv0_p, h0_p, *mixer_w, seg=MIXER_ROWS, nseg=1, carry=True,
                            emit_vrows=False, name=f"mixer_prompt_{l}")
        hs, cs, ls, vs = _mixer(xs, _pad_conv_state(state_conv[l]),
                                state_lru[l].reshape(dec_batch, HEADS, HEAD_DIM), *mixer_w,
                                seg=dec_seq, nseg=MIXER_ROWS // dec_seq, carry=False,
                                emit_vrows=True, name=f"mixer_sample_{l}")
        xp = _ffn(hp, *ffn_w, final=final, name=f"ffn_prompt_{l}")
        xs = _ffn(hs, *ffn_w, final=final, name=f"ffn_sample_{l}")

        keep = slice(CARRY_ROWS - (CONV_WIDTH - 1), CARRY_ROWS)
        conv_p.append(cp[:, keep])
        lru_p.append(lp.reshape(batch, gw))
        conv_s.append(cs[:, keep])
        lru_s.append(ls.reshape(dec_batch, gw))
        vrows_s.append(vs.reshape(dec_batch, dec_seq, gw))

    return (xp.reshape(batch, seq, D_MODEL), xs.reshape(dec_batch, dec_seq, D_MODEL),
            jnp.stack(conv_p), jnp.stack(lru_p), jnp.stack(conv_s), jnp.stack(lru_s),
            jnp.stack(vrows_s))
```

```python
import functools
import math

import jax
import jax.numpy as jnp
from jax import lax
from jax.experimental import pallas as pl
from jax.experimental.pallas import tpu as pltpu

D_MODEL = 2048
GROUP_WIDTH = D_MODEL // 2
HEADS = 8
HEAD_DIM = GROUP_WIDTH // HEADS
CONV_WIDTH = 4
MLP_CHUNK = 128
LRU_C = 8.0
EPS = 1e-6

SUBLANES = 8
CARRY_ROWS = SUBLANES
PAIR = 2 * HEAD_DIM
NPAIR = GROUP_WIDTH // PAIR
SCAN_PARTS = 4
OUT_COLS = 512
OUT_CHUNKS = D_MODEL // OUT_COLS

MIXER_ROWS = 256
FFN_ROWS = 512
FFN_COLS = 512
VMEM_LIMIT = 56 * 1024 * 1024

_CW0, _CB, _BR, _BI, _LAM, _LNG, _LNB, _GNA, _GNB = 0, 4, 5, 6, 7, 8, 9, 10, 11
_CVEC_ROWS = 16

_BF16 = jnp.bfloat16
_F32 = jnp.float32


def _dot(a, b):
    return jnp.dot(a, b, preferred_element_type=_F32)


def _rms(x, g):
    return x * lax.rsqrt(jnp.mean(x * x, axis=-1, keepdims=True) + EPS) * g


def _gelu(x):
    c = math.sqrt(2.0 / math.pi)
    return x * (0.5 * (1.0 + jnp.tanh(c * (x + 0.044715 * (x * x * x)))))


def _sigmoid(x):
    return 1.0 / (1.0 + jnp.exp(-x))


def _rowsum(parts):
    total = jnp.sum(parts[0], axis=-1, keepdims=True)
    for p in parts[1:]:
        total = total + jnp.sum(p, axis=-1, keepdims=True)
    return total


def _mixer_kernel(x_ref, conv0_ref, h0_ref, norm1_ref, cvec_ref, win_ref, wgate_ref, wsp_ref,
                  bsb_ref, wout_ref, *rest, seg, nseg, carry, emit_vrows):
    if emit_vrows:
        y_ref, convo_ref, ho_ref, vrows_ref, xpad, a3, b3, h3, hcar = rest
    else:
        y_ref, convo_ref, ho_ref, xpad, a3, b3, h3, hcar = rest
        vrows_ref = None
    rows = seg * nseg
    gw = GROUP_WIDTH
    groups_per_seg = seg // SUBLANES

    def vec(k, cols=slice(None)):
        return cvec_ref[k:k + 1, cols]

    def pair_cols(p):
        return slice(p * PAIR, (p + 1) * PAIR)

    if carry:
        t = pl.program_id(1)

        @pl.when(t == 0)
        def _():
            xpad[0, 0:CARRY_ROWS, :] = conv0_ref[0]
            hcar[...] = h0_ref[0]

        @pl.when(t > 0)
        def _():
            xpad[0, 0:CARRY_ROWS, :] = xpad[0, seg:seg + CARRY_ROWS, :]
    else:
        for s in range(nseg):
            xpad[s, 0:CARRY_ROWS, :] = conv0_ref[s]

    x = x_ref[...]
    xnb = _rms(x, norm1_ref[...]).astype(_BF16)

    def proj(group, p):
        return _dot(xnb, win_ref[group * NPAIR + p])

    def conv(p, xa_p):
        cols = pair_cols(p)
        for s in range(nseg):
            xpad[s, CARRY_ROWS:CARRY_ROWS + seg, cols] = xa_p[s * seg:(s + 1) * seg]
            convo_ref[s, :, cols] = xa_p[(s + 1) * seg - CARRY_ROWS:(s + 1) * seg]

        def shifted(k):
            parts = [xpad[s, CARRY_ROWS - k:CARRY_ROWS - k + seg, cols] for s in range(nseg)]
            return parts[0] if nseg == 1 else jnp.concatenate(parts, axis=0)

        return (shifted(3) * vec(_CW0, cols) + shifted(2) * vec(_CW0 + 1, cols)
                + shifted(1) * vec(_CW0 + 2, cols) + xa_p * vec(_CW0 + 3, cols) + vec(_CB, cols))

    def gates(p, xc_p):
        xcb = xc_p.astype(_BF16)
        g0 = _dot(xcb[:, :HEAD_DIM], wgate_ref[2 * p])
        g1 = _dot(xcb[:, HEAD_DIM:], wgate_ref[2 * p + 1])
        return (jnp.concatenate([g0[:, :HEAD_DIM], g1[:, :HEAD_DIM]], axis=1),
                jnp.concatenate([g0[:, HEAD_DIM:], g1[:, HEAD_DIM:]], axis=1))

    neg_lam = -vec(_LAM)
    softplus = jnp.maximum(neg_lam, 0.0) + jnp.log1p(jnp.exp(-jnp.abs(neg_lam)))

    def lru_coeffs(p, xc_p, g_r, g_i):
        cols = pair_cols(p)
        r = _sigmoid(g_r + vec(_BR, cols))
        ig = _sigmoid(g_i + vec(_BI, cols))
        log_a = (-LRU_C * r) * softplus[:, cols]
        a = jnp.exp(log_a)
        bterm = jnp.sqrt(-jnp.tanh(log_a) * (a * a + 1.0)) * (ig * xc_p)
        for hh in range(2):
            c = 2 * p + hh
            sub = slice(hh * HEAD_DIM, (hh + 1) * HEAD_DIM)
            a3[:, c * SUBLANES:(c + 1) * SUBLANES, :] = a[:, sub].reshape(rows // SUBLANES, SUBLANES, HEAD_DIM)
            b3[:, c * SUBLANES:(c + 1) * SUBLANES, :] = bterm[:, sub].reshape(rows // SUBLANES, SUBLANES, HEAD_DIM)

    def scan_part(q, hs):
        per = groups_per_seg // SCAN_PARTS
        hs = list(hs)
        for j in range(q * per, (q + 1) * per):
            for rr in range(SUBLANES):
                for s in range(nseg):
                    jj = s * groups_per_seg + j
                    at = a3[jj, pl.ds(rr, HEADS, stride=SUBLANES), :]
                    bt = b3[jj, pl.ds(rr, HEADS, stride=SUBLANES), :]
                    hs[s] = at * hs[s] + bt
                    h3[jj, pl.ds(rr, HEADS, stride=SUBLANES), :] = hs[s]
        return hs

    def y_lru_pair(p):
        return jnp.concatenate(
            [h3[:, c * SUBLANES:(c + 1) * SUBLANES, :].reshape(rows, HEAD_DIM) for c in (2 * p, 2 * p + 1)],
            axis=1)

    def layernorm_v(vg):
        mu = _rowsum(vg) * (1.0 / gw)
        vc = [g - mu for g in vg]
        rstd = lax.rsqrt(_rowsum([c * c for c in vc]) * (1.0 / gw) + EPS)
        return [vc[p] * rstd * vec(_LNG, pair_cols(p)) + vec(_LNB, pair_cols(p)) for p in range(NPAIR)]

    chunk = min(seg, MLP_CHUNK)
    nchunk = rows // chunk
    tri = (lax.broadcasted_iota(jnp.int32, (MLP_CHUNK, MLP_CHUNK), 0)
           >= lax.broadcasted_iota(jnp.int32, (MLP_CHUNK, MLP_CHUNK), 1))

    def token_mlp(h, vb, gu):
        p, hh = divmod(h, 2)
        sub = slice(hh * HEAD_DIM, (hh + 1) * HEAD_DIM)
        w = jnp.where(tri, wsp_ref[h], 0.0).astype(_BF16)[:chunk, :chunk]
        vh = jnp.concatenate([vb[p][k * chunk:(k + 1) * chunk, sub] for k in range(nchunk)], axis=1)
        m = _dot(w, vh)
        bias = bsb_ref[h][:chunk]
        mixed = jnp.concatenate(
            [m[:, k * HEAD_DIM:(k + 1) * HEAD_DIM] + bias for k in range(nchunk)], axis=0)
        return gu[p][:, sub] * mixed

    xa0 = proj(0, 0)
    xa1 = proj(0, 1)
    xc0 = conv(0, xa0)
    xa2 = proj(0, 2)
    xc1 = conv(1, xa1)
    gt0 = gates(0, xc0)
    xa3 = proj(0, 3)
    xc2 = conv(2, xa2)
    gt1 = gates(1, xc1)
    u = [None] * NPAIR
    u[0] = proj(2, 0)
    xc3 = conv(3, xa3)
    gt2 = gates(2, xc2)
    u[1] = proj(2, 1)
    lru_coeffs(0, xc0, *gt0)
    gt3 = gates(3, xc3)
    u[2] = proj(2, 2)
    lru_coeffs(1, xc1, *gt1)
    u[3] = proj(2, 3)
    lru_coeffs(2, xc2, *gt2)
    v = [None] * NPAIR
    v[0] = proj(3, 0)
    lru_coeffs(3, xc3, *gt3)

    hs = [hcar[...] if carry else h0_ref[s] for s in range(nseg)]
    gu = [None] * NPAIR
    vg = [None] * NPAIR
    v[1] = proj(3, 1)
    hs = scan_part(0, hs)
    gu[0] = _gelu(u[0])
    gu[1] = _gelu(u[1])
    v[2] = proj(3, 2)
    hs = scan_part(1, hs)
    gu[2] = _gelu(u[2])
    gu[3] = _gelu(u[3])
    v[3] = proj(3, 3)
    hs = scan_part(2, hs)
    vg[0] = _gelu(v[0])
    vg[1] = _gelu(v[1])
    ga = [None] * NPAIR
    ga[0] = proj(1, 0)
    hs = scan_part(3, hs)
    if carry:
        hcar[...] = hs[0]
    for s in range(nseg):
        ho_ref[s] = hs[s]
    vg[2] = _gelu(v[2])
    vg[3] = _gelu(v[3])

    ga[1] = proj(1, 1)
    v_n = layernorm_v(vg)
    if emit_vrows:
        for p in range(NPAIR):
            vrows_ref[:, pair_cols(p)] = v_n[p]
    vb = [n.astype(_BF16) for n in v_n]
    ga[2] = proj(1, 2)
    gga = [_gelu(ga[0]), _gelu(ga[1]), None, None]
    out_b = [token_mlp(h, vb, gu) for h in range(HEADS // 2)]
    ga[3] = proj(1, 3)
    gga[2] = _gelu(ga[2])
    out_b += [token_mlp(h, vb, gu) for h in range(HEADS // 2, HEADS)]
    gga[3] = _gelu(ga[3])
    out_a = [y_lru_pair(p) * gga[p] for p in range(NPAIR)]
    rs_a = lax.rsqrt(_rowsum([o * o for o in out_a]) * (1.0 / gw) + EPS)
    na = jnp.concatenate([out_a[p] * rs_a * vec(_GNA, pair_cols(p)) for p in range(NPAIR)],
                         axis=1).astype(_BF16)
    acc = [_dot(na, wout_ref[n, 0:gw, :]) for n in range(OUT_CHUNKS)]
    rs_b = lax.rsqrt(_rowsum([o * o for o in out_b]) * (1.0 / gw) + EPS)
    nb = jnp.concatenate(
        [out_b[h] * rs_b * vec(_GNB, slice(h * HEAD_DIM, (h + 1) * HEAD_DIM)) for h in range(HEADS)],
        axis=1).astype(_BF16)
    for n in range(OUT_CHUNKS):
        cols = slice(n * OUT_COLS, (n + 1) * OUT_COLS)
        y_ref[:, cols] = x[:, cols] + acc[n] + _dot(nb, wout_ref[n, gw:2 * gw, :])


def _layer_spec(shape, layer):
    zeros = (0,) * (len(shape) - 1)
    return pl.BlockSpec((None,) + tuple(shape[1:]), lambda *_: (layer,) + zeros,
                        pipeline_mode=pl.Buffered(1))


def _mixer(x2d, conv0, h0, norm1, cvec, win, wgate, wsp, bsb, wout, *, layer, state_layer, seg, nseg,
           carry, emit_vrows, name):
    rows = seg * nseg
    total = x2d.shape[0]
    nseq = conv0.shape[1]
    gw = GROUP_WIDTH
    if carry:
        steps = total // nseq // rows
        grid = (nseq, steps)
        row_map = lambda b, t: (b * steps + t, 0)
        seq_map = lambda b, t: (b, 0, 0)
        state_map = lambda b, t: (state_layer, b, 0, 0)
    else:
        grid = (total // rows,)
        row_map = lambda i: (i, 0)
        seq_map = lambda i: (i, 0, 0)
        state_map = lambda i: (state_layer, i, 0, 0)
    out_shape = [jax.ShapeDtypeStruct((total, D_MODEL), _F32),
                 jax.ShapeDtypeStruct((nseq, CARRY_ROWS, gw), _F32),
                 jax.ShapeDtypeStruct((nseq, HEADS, HEAD_DIM), _F32)]
    out_specs = [pl.BlockSpec((rows, D_MODEL), row_map),
                 pl.BlockSpec((nseg, CARRY_ROWS, gw), seq_map),
                 pl.BlockSpec((nseg, HEADS, HEAD_DIM), seq_map)]
    if emit_vrows:
        out_shape.append(jax.ShapeDtypeStruct((total, gw), _F32))
        out_specs.append(pl.BlockSpec((rows, gw), row_map))
    scan_shape = (rows // SUBLANES, HEADS * SUBLANES, HEAD_DIM)
    return pl.pallas_call(
        functools.partial(_mixer_kernel, seg=seg, nseg=nseg, carry=carry, emit_vrows=emit_vrows),
        grid=grid,
        in_specs=[pl.BlockSpec((rows, D_MODEL), row_map),
                  pl.BlockSpec((None, nseg, CARRY_ROWS, gw), state_map),
                  pl.BlockSpec((None, nseg, HEADS, HEAD_DIM), state_map),
                  _layer_spec(norm1.shape, layer), _layer_spec(cvec.shape, layer),
                  _layer_spec(win.shape, layer), _layer_spec(wgate.shape, layer),
                  _layer_spec(wsp.shape, layer), _layer_spec(bsb.shape, layer),
                  _layer_spec(wout.shape, layer)],
        out_specs=out_specs,
        out_shape=out_shape,
        scratch_shapes=[pltpu.VMEM((nseg, seg + CARRY_ROWS, gw), _F32),
                        pltpu.VMEM(scan_shape, _F32), pltpu.VMEM(scan_shape, _F32),
                        pltpu.VMEM(scan_shape, _F32), pltpu.VMEM((HEADS, HEAD_DIM), _F32)],
        compiler_params=pltpu.CompilerParams(
            dimension_semantics=("arbitrary",) * len(grid), vmem_limit_bytes=VMEM_LIMIT),
        name=name,
    )(x2d, conv0, h0, norm1, cvec, win, wgate, wsp, bsb, wout)


def _ffn_kernel(h_ref, norm2_ref, wg_ref, wu_ref, wd_ref, normf_ref, o_ref, hn_ref, acc_ref, *, final):
    j = pl.program_id(1)

    @pl.when(j == 0)
    def _():
        hn_ref[...] = _rms(h_ref[...], norm2_ref[...]).astype(_BF16)
        acc_ref[...] = jnp.zeros_like(acc_ref)

    hn = hn_ref[...]
    g = _dot(hn, wg_ref[...])
    u = _dot(hn, wu_ref[...])
    act = ((g * _sigmoid(g)) * u).astype(_BF16)
    acc_ref[...] += _dot(act, wd_ref[...])

    @pl.when(j == pl.num_programs(1) - 1)
    def _():
        y = h_ref[...] + acc_ref[...]
        if final:
            y = _rms(y, normf_ref[...])
        o_ref[...] = y


def _ffn(h2d, norm2, wg, wu, wd, normf, *, layer, final, name):
    total = h2d.shape[0]
    d_ff = wg.shape[-1]
    rows = min(FFN_ROWS, total)
    grid = (total // rows, d_ff // FFN_COLS)
    return pl.pallas_call(
        functools.partial(_ffn_kernel, final=final),
        grid=grid,
        in_specs=[pl.BlockSpec((rows, D_MODEL), lambda i, j: (i, 0)),
                  _layer_spec(norm2.shape, layer),
                  pl.BlockSpec((None, D_MODEL, FFN_COLS), lambda i, j: (layer, 0, j)),
                  pl.BlockSpec((None, D_MODEL, FFN_COLS), lambda i, j: (layer, 0, j)),
                  pl.BlockSpec((None, FFN_COLS, D_MODEL), lambda i, j: (layer, j, 0)),
                  _layer_spec(normf.shape, 0)],
        out_specs=pl.BlockSpec((rows, D_MODEL), lambda i, j: (i, 0)),
        out_shape=jax.ShapeDtypeStruct((total, D_MODEL), _F32),
        scratch_shapes=[pltpu.VMEM((rows, D_MODEL), _BF16), pltpu.VMEM((rows, D_MODEL), _F32)],
        compiler_params=pltpu.CompilerParams(
            dimension_semantics=("arbitrary", "arbitrary"), vmem_limit_bytes=VMEM_LIMIT),
        name=name,
    )(h2d, norm2, wg, wu, wd, normf)


def kernel(x_prompt, x_sample, state_conv, state_lru, norm1, w_in, conv_w, conv_b, w_rgate, b_rgate,
           w_igate, b_igate, lru_param, v_ln_g, v_ln_b, w_spatial, b_spatial, gn_a, gn_b, w_out,
           norm2, w_gate, w_up, w_down, norm_f):
    depth = w_in.shape[0]
    batch, seq, _ = x_prompt.shape
    dec_batch, dec_seq, _ = x_sample.shape
    gw = GROUP_WIDTH

    xp = x_prompt.reshape(batch * seq, D_MODEL)
    xs = x_sample.reshape(dec_batch * dec_seq, D_MODEL)
    conv0_p = jnp.zeros((1, batch, CARRY_ROWS, gw), _F32)
    h0_p = jnp.zeros((1, batch, HEADS, HEAD_DIM), _F32)
    conv0_s = jnp.pad(state_conv, ((0, 0), (0, 0), (CARRY_ROWS - (CONV_WIDTH - 1), 0), (0, 0)))
    h0_s = state_lru.reshape(depth, dec_batch, HEADS, HEAD_DIM)

    row = lambda a: a[:, None, :]
    cvec = jnp.concatenate(
        [conv_w, row(conv_b), row(b_rgate), row(b_igate), row(lru_param), row(v_ln_g), row(v_ln_b),
         row(gn_a), row(gn_b), jnp.zeros((depth, _CVEC_ROWS - 12, gw), _F32)], axis=1)
    wgate = jnp.concatenate([w_rgate, w_igate], axis=-1).astype(_BF16)
    bsb = jnp.broadcast_to(b_spatial[..., None], (depth, HEADS, MLP_CHUNK, HEAD_DIM))
    win = w_in.astype(_BF16).reshape(depth, D_MODEL, 4 * NPAIR, PAIR).transpose(0, 2, 1, 3)
    wout = w_out.astype(_BF16).reshape(depth, D_MODEL, OUT_CHUNKS, OUT_COLS).transpose(0, 2, 1, 3)
    mixer_w = (norm1[:, None, :], cvec, win, wgate, w_spatial, bsb, wout)
    ffn_w = (norm2[:, None, :], w_gate.astype(_BF16), w_up.astype(_BF16), w_down.astype(_BF16),
             norm_f.reshape(1, 1, D_MODEL))

    conv_p, lru_p, conv_s, lru_s, vrows_s = [], [], [], [], []
    for l in range(depth):
        final = l == depth - 1
        hp, cp, lp = _mixer(xp, conv0_p, h0_p, *mixer_w, layer=l, state_layer=0, seg=MIXER_ROWS,
                            nseg=1, carry=True, emit_vrows=False, name=f"mixer_prompt_{l}")
        hs, cs, ls, vs = _mixer(xs, conv0_s, h0_s, *mixer_w, layer=l, state_layer=l, seg=dec_seq,
                                nseg=MIXER_ROWS // dec_seq, carry=False, emit_vrows=True,
                                name=f"mixer_sample_{l}")
        xp = _ffn(hp, *ffn_w, layer=l, final=final, name=f"ffn_prompt_{l}")
        xs = _ffn(hs, *ffn_w, layer=l, final=final, name=f"ffn_sample_{l}")

        keep = slice(CARRY_ROWS - (CONV_WIDTH - 1), CARRY_ROWS)
        conv_p.append(cp[:, keep])
        lru_p.append(lp.reshape(batch, gw))
        conv_s.append(cs[:, keep])
        lru_s.append(ls.reshape(dec_batch, gw))
        vrows_s.append(vs.reshape(dec_batch, dec_seq, gw))

    return (xp.reshape(batch, seq, D_MODEL), xs.reshape(dec_batch, dec_seq, D_MODEL),
            jnp.stack(conv_p), jnp.stack(lru_p), jnp.stack(conv_s), jnp.stack(lru_s),
            jnp.stack(vrows_s))
```

```python
import functools
import math

import jax
import jax.numpy as jnp
from jax import lax
from jax.experimental import pallas as pl
from jax.experimental.pallas import tpu as pltpu

D_MODEL = 2048
GROUP_WIDTH = D_MODEL // 2
HEADS = 8
HEAD_DIM = GROUP_WIDTH // HEADS
CONV_WIDTH = 4
MLP_CHUNK = 128
LRU_C = 8.0
EPS = 1e-6

SUBLANES = 8
CARRY_ROWS = SUBLANES
PAIR = 2 * HEAD_DIM
NPAIR = GROUP_WIDTH // PAIR
SCAN_PARTS = 4
OUT_COLS = 512
OUT_CHUNKS = D_MODEL // OUT_COLS

MIXER_ROWS = 256
FFN_ROWS = 1024
FFN_COLS = 512
VMEM_LIMIT = 60 * 1024 * 1024

_CW0, _CB, _BR, _BI, _LAM, _LNG, _LNB, _GNA, _GNB = 0, 4, 5, 6, 7, 8, 9, 10, 11
_CVEC_ROWS = 16

_BF16 = jnp.bfloat16
_F32 = jnp.float32
_MIN_NORMAL = float(jnp.finfo(jnp.float32).tiny)


def _dot(a, b):
    return jnp.dot(a, b, preferred_element_type=_F32)


def _rms(x, g):
    return x * lax.rsqrt(jnp.mean(x * x, axis=-1, keepdims=True) + EPS) * g


def _gelu(x):
    c = math.sqrt(2.0 / math.pi)
    hx = 0.5 * x
    return hx + hx * jnp.tanh(x * (c + (c * 0.044715) * (x * x)))


def _sigmoid(x):
    return 1.0 / (1.0 + jnp.exp(-x))


def _rowsum(parts):
    total = jnp.sum(parts[0], axis=-1, keepdims=True)
    for p in parts[1:]:
        total = total + jnp.sum(p, axis=-1, keepdims=True)
    return total


def _mixer_kernel(x_ref, conv0_ref, h0_ref, norm1_ref, cvec_ref, win_ref, wgate_ref, wsp_ref,
                  bsb_ref, wout_ref, *rest, seg, nseg, carry, emit_vrows):
    if emit_vrows:
        y_ref, convo_ref, ho_ref, vrows_ref, xpad, a3, b3, h3, hcar = rest
    else:
        y_ref, convo_ref, ho_ref, xpad, a3, b3, h3, hcar = rest
        vrows_ref = None
    rows = seg * nseg
    gw = GROUP_WIDTH
    groups_per_seg = seg // SUBLANES

    def vec(k, cols=slice(None)):
        return cvec_ref[k:k + 1, cols]

    def pair_cols(p):
        return slice(p * PAIR, (p + 1) * PAIR)

    if carry:
        t = pl.program_id(1)

        @pl.when(t == 0)
        def _():
            xpad[0, 0:CARRY_ROWS, :] = conv0_ref[0]
            hcar[...] = h0_ref[0]

        @pl.when(t > 0)
        def _():
            xpad[0, 0:CARRY_ROWS, :] = xpad[0, seg:seg + CARRY_ROWS, :]
    else:
        for s in range(nseg):
            xpad[s, 0:CARRY_ROWS, :] = conv0_ref[s]

    x = x_ref[...]
    xnb = _rms(x, norm1_ref[...]).astype(_BF16)

    def proj(group, p):
        return _dot(xnb, win_ref[group * NPAIR + p])

    def conv(p, xa_p):
        cols = pair_cols(p)
        for s in range(nseg):
            xpad[s, CARRY_ROWS:CARRY_ROWS + seg, cols] = xa_p[s * seg:(s + 1) * seg]
            convo_ref[s, :, cols] = xa_p[(s + 1) * seg - CARRY_ROWS:(s + 1) * seg]

        def shifted(k):
            parts = [xpad[s, CARRY_ROWS - k:CARRY_ROWS - k + seg, cols] for s in range(nseg)]
            return parts[0] if nseg == 1 else jnp.concatenate(parts, axis=0)

        return (shifted(3) * vec(_CW0, cols) + shifted(2) * vec(_CW0 + 1, cols)
                + shifted(1) * vec(_CW0 + 2, cols) + xa_p * vec(_CW0 + 3, cols) + vec(_CB, cols))

    def gates(p, xc_p):
        xcb = xc_p.astype(_BF16)
        g0 = _dot(xcb[:, :HEAD_DIM], wgate_ref[2 * p])
        g1 = _dot(xcb[:, HEAD_DIM:], wgate_ref[2 * p + 1])
        return (jnp.concatenate([g0[:, :HEAD_DIM], g1[:, :HEAD_DIM]], axis=1),
                jnp.concatenate([g0[:, HEAD_DIM:], g1[:, HEAD_DIM:]], axis=1))

    neg_lam = -vec(_LAM)
    softplus = jnp.maximum(neg_lam, 0.0) + jnp.log1p(jnp.exp(-jnp.abs(neg_lam)))
    neg_c_softplus = -LRU_C * softplus

    def lru_coeffs(p, xc_p, g_r, g_i):
        cols = pair_cols(p)
        r = _sigmoid(g_r + vec(_BR, cols))
        ig = _sigmoid(g_i + vec(_BI, cols))
        log_a = r * neg_c_softplus[:, cols]
        a = jnp.exp(log_a)
        m = -jnp.tanh(log_a) * (a * a + 1.0)
        bterm = (m * lax.rsqrt(jnp.maximum(m, _MIN_NORMAL))) * (ig * xc_p)
        for hh in range(2):
            c = 2 * p + hh
            sub = slice(hh * HEAD_DIM, (hh + 1) * HEAD_DIM)
            a3[:, c * SUBLANES:(c + 1) * SUBLANES, :] = a[:, sub].reshape(rows // SUBLANES, SUBLANES, HEAD_DIM)
            b3[:, c * SUBLANES:(c + 1) * SUBLANES, :] = bterm[:, sub].reshape(rows // SUBLANES, SUBLANES, HEAD_DIM)

    def scan_part(q, hs):
        per = groups_per_seg // SCAN_PARTS
        hs = list(hs)
        for j in range(q * per, (q + 1) * per):
            for rr in range(SUBLANES):
                for s in range(nseg):
                    jj = s * groups_per_seg + j
                    at = a3[jj, pl.ds(rr, HEADS, stride=SUBLANES), :]
                    bt = b3[jj, pl.ds(rr, HEADS, stride=SUBLANES), :]
                    hs[s] = at * hs[s] + bt
                    h3[jj, pl.ds(rr, HEADS, stride=SUBLANES), :] = hs[s]
        return hs

    def y_lru_pair(p):
        return jnp.concatenate(
            [h3[:, c * SUBLANES:(c + 1) * SUBLANES, :].reshape(rows, HEAD_DIM) for c in (2 * p, 2 * p + 1)],
            axis=1)

    def layernorm_v(vg):
        mu = _rowsum(vg) * (1.0 / gw)
        vc = [g - mu for g in vg]
        rstd = lax.rsqrt(_rowsum([c * c for c in vc]) * (1.0 / gw) + EPS)
        return [vc[p] * rstd * vec(_LNG, pair_cols(p)) + vec(_LNB, pair_cols(p)) for p in range(NPAIR)]

    chunk = min(seg, MLP_CHUNK)
    nchunk = rows // chunk
    tri = (lax.broadcasted_iota(jnp.int32, (MLP_CHUNK, MLP_CHUNK), 0)
           >= lax.broadcasted_iota(jnp.int32, (MLP_CHUNK, MLP_CHUNK), 1))

    def token_mlp(h, vb, gu):
        p, hh = divmod(h, 2)
        sub = slice(hh * HEAD_DIM, (hh + 1) * HEAD_DIM)
        w = jnp.where(tri, wsp_ref[h], 0.0).astype(_BF16)[:chunk, :chunk]
        vh = jnp.concatenate([vb[p][k * chunk:(k + 1) * chunk, sub] for k in range(nchunk)], axis=1)
        m = _dot(w, vh)
        bias = bsb_ref[h][:chunk]
        mixed = jnp.concatenate(
            [m[:, k * HEAD_DIM:(k + 1) * HEAD_DIM] + bias for k in range(nchunk)], axis=0)
        return gu[p][:, sub] * mixed

    xa0 = proj(0, 0)
    xa1 = proj(0, 1)
    xc0 = conv(0, xa0)
    xa2 = proj(0, 2)
    xc1 = conv(1, xa1)
    gt0 = gates(0, xc0)
    xa3 = proj(0, 3)
    xc2 = conv(2, xa2)
    gt1 = gates(1, xc1)
    u = [None] * NPAIR
    u[0] = proj(2, 0)
    xc3 = conv(3, xa3)
    gt2 = gates(2, xc2)
    u[1] = proj(2, 1)
    lru_coeffs(0, xc0, *gt0)
    gt3 = gates(3, xc3)
    u[2] = proj(2, 2)
    lru_coeffs(1, xc1, *gt1)
    u[3] = proj(2, 3)
    lru_coeffs(2, xc2, *gt2)
    v = [None] * NPAIR
    v[0] = proj(3, 0)
    lru_coeffs(3, xc3, *gt3)

    hs = [hcar[...] if carry else h0_ref[s] for s in range(nseg)]
    gu = [None] * NPAIR
    vg = [None] * NPAIR
    v[1] = proj(3, 1)
    hs = scan_part(0, hs)
    gu[0] = _gelu(u[0])
    gu[1] = _gelu(u[1])
    v[2] = proj(3, 2)
    hs = scan_part(1, hs)
    gu[2] = _gelu(u[2])
    gu[3] = _gelu(u[3])
    v[3] = proj(3, 3)
    hs = scan_part(2, hs)
    vg[0] = _gelu(v[0])
    vg[1] = _gelu(v[1])
    ga = [None] * NPAIR
    ga[0] = proj(1, 0)
    hs = scan_part(3, hs)
    if carry:
        hcar[...] = hs[0]
    for s in range(nseg):
        ho_ref[s] = hs[s]
    vg[2] = _gelu(v[2])
    vg[3] = _gelu(v[3])

    ga[1] = proj(1, 1)
    v_n = layernorm_v(vg)
    if emit_vrows:
        for p in range(NPAIR):
            vrows_ref[:, pair_cols(p)] = v_n[p]
    vb = [n.astype(_BF16) for n in v_n]
    ga[2] = proj(1, 2)
    gga = [_gelu(ga[0]), _gelu(ga[1]), None, None]
    out_b = [token_mlp(h, vb, gu) for h in range(HEADS // 2)]
    ga[3] = proj(1, 3)
    gga[2] = _gelu(ga[2])
    out_b += [token_mlp(h, vb, gu) for h in range(HEADS // 2, HEADS)]
    gga[3] = _gelu(ga[3])
    out_a = [y_lru_pair(p) * gga[p] for p in range(NPAIR)]
    rs_a = lax.rsqrt(_rowsum([o * o for o in out_a]) * (1.0 / gw) + EPS)
    na = jnp.concatenate([out_a[p] * rs_a * vec(_GNA, pair_cols(p)) for p in range(NPAIR)],
                         axis=1).astype(_BF16)
    acc = [_dot(na, wout_ref[n, 0:gw, :]) for n in range(OUT_CHUNKS)]
    rs_b = lax.rsqrt(_rowsum([o * o for o in out_b]) * (1.0 / gw) + EPS)
    nb = jnp.concatenate(
        [out_b[h] * rs_b * vec(_GNB, slice(h * HEAD_DIM, (h + 1) * HEAD_DIM)) for h in range(HEADS)],
        axis=1).astype(_BF16)
    for n in range(OUT_CHUNKS):
        cols = slice(n * OUT_COLS, (n + 1) * OUT_COLS)
        y_ref[:, cols] = x[:, cols] + acc[n] + _dot(nb, wout_ref[n, gw:2 * gw, :])


def _layer_spec(shape, layer):
    zeros = (0,) * (len(shape) - 1)
    return pl.BlockSpec((None,) + tuple(shape[1:]), lambda *_: (layer,) + zeros,
                        pipeline_mode=pl.Buffered(1))


def _mixer(x2d, conv0, h0, norm1, cvec, win, wgate, wsp, bsb, wout, *, layer, state_layer, seg, nseg,
           carry, emit_vrows, name):
    rows = seg * nseg
    total = x2d.shape[0]
    nseq = conv0.shape[1]
    gw = GROUP_WIDTH
    if carry:
        steps = total // nseq // rows
        grid = (nseq, steps)
        row_map = lambda b, t: (b * steps + t, 0)
        seq_map = lambda b, t: (b, 0, 0)
        state_map = lambda b, t: (state_layer, b, 0, 0)
    else:
        grid = (total // rows,)
        row_map = lambda i: (i, 0)
        seq_map = lambda i: (i, 0, 0)
        state_map = lambda i: (state_layer, i, 0, 0)
    out_shape = [jax.ShapeDtypeStruct((total, D_MODEL), _F32),
                 jax.ShapeDtypeStruct((nseq, CARRY_ROWS, gw), _F32),
                 jax.ShapeDtypeStruct((nseq, HEADS, HEAD_DIM), _F32)]
    out_specs = [pl.BlockSpec((rows, D_MODEL), row_map),
                 pl.BlockSpec((nseg, CARRY_ROWS, gw), seq_map),
                 pl.BlockSpec((nseg, HEADS, HEAD_DIM), seq_map)]
    if emit_vrows:
        out_shape.append(jax.ShapeDtypeStruct((total, gw), _F32))
        out_specs.append(pl.BlockSpec((rows, gw), row_map))
    scan_shape = (rows // SUBLANES, HEADS * SUBLANES, HEAD_DIM)
    return pl.pallas_call(
        functools.partial(_mixer_kernel, seg=seg, nseg=nseg, carry=carry, emit_vrows=emit_vrows),
        grid=grid,
        in_specs=[pl.BlockSpec((rows, D_MODEL), row_map),
                  pl.BlockSpec((None, nseg, CARRY_ROWS, gw), state_map),
                  pl.BlockSpec((None, nseg, HEADS, HEAD_DIM), state_map),
                  _layer_spec(norm1.shape, layer), _layer_spec(cvec.shape, layer),
                  _layer_spec(win.shape, layer), _layer_spec(wgate.shape, layer),
                  _layer_spec(wsp.shape, layer), _layer_spec(bsb.shape, layer),
                  _layer_spec(wout.shape, layer)],
        out_specs=out_specs,
        out_shape=out_shape,
        scratch_shapes=[pltpu.VMEM((nseg, seg + CARRY_ROWS, gw), _F32),
                        pltpu.VMEM(scan_shape, _F32), pltpu.VMEM(scan_shape, _F32),
                        pltpu.VMEM(scan_shape, _F32), pltpu.VMEM((HEADS, HEAD_DIM), _F32)],
        compiler_params=pltpu.CompilerParams(
            dimension_semantics=("arbitrary",) * len(grid), vmem_limit_bytes=VMEM_LIMIT),
        name=name,
    )(x2d, conv0, h0, norm1, cvec, win, wgate, wsp, bsb, wout)


def _ffn_kernel(h_ref, norm2_ref, wg_ref, wu_ref, wd_ref, normf_ref, o_ref, hn_ref, *, final):
    j = pl.program_id(1)

    @pl.when(j == 0)
    def _():
        h = h_ref[...]
        hn_ref[...] = _rms(h, norm2_ref[...]).astype(_BF16)
        o_ref[...] = h

    hn = hn_ref[...]
    g = _dot(hn, wg_ref[...])
    u = _dot(hn, wu_ref[...])
    act = ((g * _sigmoid(g)) * u).astype(_BF16)
    for n in range(OUT_CHUNKS):
        cols = slice(n * OUT_COLS, (n + 1) * OUT_COLS)
        o_ref[:, cols] += _dot(act, wd_ref[n])

    if final:
        @pl.when(j == pl.num_programs(1) - 1)
        def _():
            o_ref[...] = _rms(o_ref[...], normf_ref[...])


def _ffn(h2d, norm2, wg, wu, wd, normf, *, layer, final, name):
    total = h2d.shape[0]
    d_ff = wg.shape[-1]
    rows = min(FFN_ROWS, total)
    grid = (total // rows, d_ff // FFN_COLS)
    return pl.pallas_call(
        functools.partial(_ffn_kernel, final=final),
        grid=grid,
        in_specs=[pl.BlockSpec((rows, D_MODEL), lambda i, j: (i, 0)),
                  _layer_spec(norm2.shape, layer),
                  pl.BlockSpec((None, D_MODEL, FFN_COLS), lambda i, j: (layer, 0, j)),
                  pl.BlockSpec((None, D_MODEL, FFN_COLS), lambda i, j: (layer, 0, j)),
                  pl.BlockSpec((None, OUT_CHUNKS, FFN_COLS, OUT_COLS), lambda i, j: (layer, 0, j, 0)),
                  _layer_spec(normf.shape, 0)],
        out_specs=pl.BlockSpec((rows, D_MODEL), lambda i, j: (i, 0)),
        out_shape=jax.ShapeDtypeStruct((total, D_MODEL), _F32),
        scratch_shapes=[pltpu.VMEM((rows, D_MODEL), _BF16)],
        compiler_params=pltpu.CompilerParams(
            dimension_semantics=("arbitrary", "arbitrary"), vmem_limit_bytes=VMEM_LIMIT),
        name=name,
    )(h2d, norm2, wg, wu, wd, normf)


def kernel(x_prompt, x_sample, state_conv, state_lru, norm1, w_in, conv_w, conv_b, w_rgate, b_rgate,
           w_igate, b_igate, lru_param, v_ln_g, v_ln_b, w_spatial, b_spatial, gn_a, gn_b, w_out,
           norm2, w_gate, w_up, w_down, norm_f):
    depth = w_in.shape[0]
    batch, seq, _ = x_prompt.shape
    dec_batch, dec_seq, _ = x_sample.shape
    gw = GROUP_WIDTH

    xp = x_prompt.reshape(batch * seq, D_MODEL)
    xs = x_sample.reshape(dec_batch * dec_seq, D_MODEL)
    conv0_p = jnp.zeros((1, batch, CARRY_ROWS, gw), _F32)
    h0_p = jnp.zeros((1, batch, HEADS, HEAD_DIM), _F32)
    conv0_s = jnp.pad(state_conv, ((0, 0), (0, 0), (CARRY_ROWS - (CONV_WIDTH - 1), 0), (0, 0)))
    h0_s = state_lru.reshape(depth, dec_batch, HEADS, HEAD_DIM)

    row = lambda a: a[:, None, :]
    cvec = jnp.concatenate(
        [conv_w, row(conv_b), row(b_rgate), row(b_igate), row(lru_param), row(v_ln_g), row(v_ln_b),
         row(gn_a), row(gn_b), jnp.zeros((depth, _CVEC_ROWS - 12, gw), _F32)], axis=1)
    wgate = jnp.concatenate([w_rgate, w_igate], axis=-1).astype(_BF16)
    bsb = jnp.broadcast_to(b_spatial[..., None], (depth, HEADS, MLP_CHUNK, HEAD_DIM))
    win = w_in.astype(_BF16).reshape(depth, D_MODEL, 4 * NPAIR, PAIR).transpose(0, 2, 1, 3)
    wout = w_out.astype(_BF16).reshape(depth, D_MODEL, OUT_CHUNKS, OUT_COLS).transpose(0, 2, 1, 3)
    mixer_w = (norm1[:, None, :], cvec, win, wgate, w_spatial, bsb, wout)
    d_ff = w_down.shape[1]
    wdown = w_down.astype(_BF16).reshape(depth, d_ff, OUT_CHUNKS, OUT_COLS).transpose(0, 2, 1, 3)
    ffn_w = (norm2[:, None, :], w_gate.astype(_BF16), w_up.astype(_BF16), wdown,
             norm_f.reshape(1, 1, D_MODEL))

    conv_p, lru_p, conv_s, lru_s, vrows_s = [], [], [], [], []
    for l in range(depth):
        final = l == depth - 1
        hp, cp, lp = _mixer(xp, conv0_p, h0_p, *mixer_w, layer=l, state_layer=0, seg=MIXER_ROWS,
                            nseg=1, carry=True, emit_vrows=False, name=f"mixer_prompt_{l}")
        hs, cs, ls, vs = _mixer(xs, conv0_s, h0_s, *mixer_w, layer=l, state_layer=l, seg=dec_seq,
                                nseg=MIXER_ROWS // dec_seq, carry=False, emit_vrows=True,
                                name=f"mixer_sample_{l}")
        xp = _ffn(hp, *ffn_w, layer=l, final=final, name=f"ffn_prompt_{l}")
        xs = _ffn(hs, *ffn_w, layer=l, final=final, name=f"ffn_sample_{l}")

        keep = slice(CARRY_ROWS - (CONV_WIDTH - 1), CARRY_ROWS)
        conv_p.append(cp[:, keep])
        lru_p.append(lp.reshape(batch, gw))
        conv_s.append(cs[:, keep])
        lru_s.append(ls.reshape(dec_batch, gw))
        vrows_s.append(vs.reshape(dec_batch, dec_seq, gw))

    return (xp.reshape(batch, seq, D_MODEL), xs.reshape(dec_batch, dec_seq, D_MODEL),
            jnp.stack(conv_p), jnp.stack(lru_p), jnp.stack(conv_s), jnp.stack(lru_s),
            jnp.stack(vrows_s))
```

```python
import functools
import math

import jax
import jax.numpy as jnp
from jax import lax
from jax.experimental import pallas as pl
from jax.experimental.pallas import tpu as pltpu

D_MODEL = 2048
GROUP_WIDTH = D_MODEL // 2
HEADS = 8
HEAD_DIM = GROUP_WIDTH // HEADS
CONV_WIDTH = 4
MLP_CHUNK = 128
LRU_C = 8.0
EPS = 1e-6

SUBLANES = 8
BF16_ROWS = 2 * SUBLANES
CARRY_ROWS = SUBLANES
PAIR = 2 * HEAD_DIM
NPAIR = GROUP_WIDTH // PAIR
SCAN_PARTS = 4
OUT_COLS = 512
OUT_CHUNKS = D_MODEL // OUT_COLS

MIXER_ROWS = 256
FFN_ROWS = 1024
FFN_COLS = 512
VMEM_LIMIT = 60 * 1024 * 1024

_CW0, _CB, _BR, _BI, _LAM, _LNG, _LNB, _GNA, _GNB = 0, 4, 5, 6, 7, 8, 9, 10, 11
_CVEC_ROWS = 16

_BF16 = jnp.bfloat16
_F32 = jnp.float32
_MIN_NORMAL = float(jnp.finfo(jnp.float32).tiny)


def _dot(a, b):
    return jnp.dot(a, b, preferred_element_type=_F32)


def _rms(x, g):
    return x * lax.rsqrt(jnp.mean(x * x, axis=-1, keepdims=True) + EPS) * g


def _gelu(x):
    c = math.sqrt(2.0 / math.pi)
    hx = 0.5 * x
    return hx + hx * jnp.tanh(x * (c + (c * 0.044715) * (x * x)))


def _sigmoid(x):
    return 1.0 / (1.0 + jnp.exp(-x))


def _rowsum(parts):
    total = jnp.sum(parts[0], axis=-1, keepdims=True)
    for p in parts[1:]:
        total = total + jnp.sum(p, axis=-1, keepdims=True)
    return total


def _convert_slab(src, dst):
    if len(dst.shape) == 2:
        dst[...] = src[...].astype(_BF16)
    else:
        width = dst.shape[2]
        for c in range(dst.shape[0]):
            dst[c] = src[:, c * width:(c + 1) * width].astype(_BF16)


def _convert_specs(casts, steps, step_of):
    in_specs, out_specs, out_shape = [], [], []
    for w, layer, width in casts:
        _, nrow, ncol = w.shape
        hold = next(h for h in (1, 2, 4, 8)
                    if nrow * h % steps == 0 and (nrow * h // steps) % BF16_ROWS == 0)
        slab = nrow * hold // steps
        index = lambda *g, hold=hold: step_of(*g) // hold
        in_specs.append(pl.BlockSpec((None, slab, ncol),
                                     lambda *g, layer=layer, index=index: (layer, index(*g), 0)))
        if width is None:
            out_shape.append(jax.ShapeDtypeStruct((nrow, ncol), _BF16))
            out_specs.append(pl.BlockSpec((slab, ncol), lambda *g, index=index: (index(*g), 0)))
        else:
            out_shape.append(jax.ShapeDtypeStruct((ncol // width, nrow, width), _BF16))
            out_specs.append(pl.BlockSpec((ncol // width, slab, width),
                                          lambda *g, index=index: (0, index(*g), 0)))
    return in_specs, out_specs, out_shape


def _convert_kernel(*refs):
    half = len(refs) // 2
    for src, dst in zip(refs[:half], refs[half:]):
        _convert_slab(src, dst)


def _convert(casts, steps, name):
    in_specs, out_specs, out_shape = _convert_specs(casts, steps, lambda i: i)
    return pl.pallas_call(
        _convert_kernel, grid=(steps,), in_specs=in_specs, out_specs=out_specs, out_shape=out_shape,
        compiler_params=pltpu.CompilerParams(
            dimension_semantics=("arbitrary",), vmem_limit_bytes=VMEM_LIMIT),
        name=name,
    )(*[w for w, _, _ in casts])


def _mixer_kernel(*refs, seg, nseg, carry, emit_vrows, ncast):
    it = iter(refs)
    take = lambda n: [next(it) for _ in range(n)]
    x_ref, conv0_ref, h0_ref, norm1_ref, cvec_ref = take(5)
    win_ref, wgate_ref, wsp_ref, bsb_ref, wout_ref = take(5)
    cast_src = take(ncast)
    y_ref, convo_ref, ho_ref = take(3)
    vrows_ref = take(1)[0] if emit_vrows else None
    cast_dst = take(ncast)
    xpad, a3, b3, h3, hcar = take(5)
    rows = seg * nseg
    gw = GROUP_WIDTH
    groups_per_seg = seg // SUBLANES

    def vec(k, cols=slice(None)):
        return cvec_ref[k:k + 1, cols]

    def pair_cols(p):
        return slice(p * PAIR, (p + 1) * PAIR)

    if carry:
        t = pl.program_id(1)

        @pl.when(t == 0)
        def _():
            xpad[0, 0:CARRY_ROWS, :] = conv0_ref[0]
            hcar[...] = h0_ref[0]

        @pl.when(t > 0)
        def _():
            xpad[0, 0:CARRY_ROWS, :] = xpad[0, seg:seg + CARRY_ROWS, :]
    else:
        for s in range(nseg):
            xpad[s, 0:CARRY_ROWS, :] = conv0_ref[s]

    x = x_ref[...]
    xnb = _rms(x, norm1_ref[...]).astype(_BF16)

    def proj(group, p):
        return _dot(xnb, win_ref[group * NPAIR + p])

    def conv(p, xa_p):
        cols = pair_cols(p)
        for s in range(nseg):
            xpad[s, CARRY_ROWS:CARRY_ROWS + seg, cols] = xa_p[s * seg:(s + 1) * seg]
            convo_ref[s, :, cols] = xa_p[(s + 1) * seg - CARRY_ROWS:(s + 1) * seg]

        def shifted(k):
            parts = [xpad[s, CARRY_ROWS - k:CARRY_ROWS - k + seg, cols] for s in range(nseg)]
            return parts[0] if nseg == 1 else jnp.concatenate(parts, axis=0)

        return (shifted(3) * vec(_CW0, cols) + shifted(2) * vec(_CW0 + 1, cols)
                + shifted(1) * vec(_CW0 + 2, cols) + xa_p * vec(_CW0 + 3, cols) + vec(_CB, cols))

    def gates(p, xc_p):
        xcb = xc_p.astype(_BF16)
        g0 = _dot(xcb[:, :HEAD_DIM], wgate_ref[2 * p])
        g1 = _dot(xcb[:, HEAD_DIM:], wgate_ref[2 * p + 1])
        return (jnp.concatenate([g0[:, :HEAD_DIM], g1[:, :HEAD_DIM]], axis=1),
                jnp.concatenate([g0[:, HEAD_DIM:], g1[:, HEAD_DIM:]], axis=1))

    neg_lam = -vec(_LAM)
    softplus = jnp.maximum(neg_lam, 0.0) + jnp.log1p(jnp.exp(-jnp.abs(neg_lam)))
    neg_c_softplus = -LRU_C * softplus

    def lru_coeffs(p, xc_p, g_r, g_i):
        cols = pair_cols(p)
        r = _sigmoid(g_r + vec(_BR, cols))
        ig = _sigmoid(g_i + vec(_BI, cols))
        log_a = r * neg_c_softplus[:, cols]
        a = jnp.exp(log_a)
        m = -jnp.tanh(log_a) * (a * a + 1.0)
        bterm = (m * lax.rsqrt(jnp.maximum(m, _MIN_NORMAL))) * (ig * xc_p)
        for hh in range(2):
            c = 2 * p + hh
            sub = slice(hh * HEAD_DIM, (hh + 1) * HEAD_DIM)
            a3[:, c * SUBLANES:(c + 1) * SUBLANES, :] = a[:, sub].reshape(rows // SUBLANES, SUBLANES, HEAD_DIM)
            b3[:, c * SUBLANES:(c + 1) * SUBLANES, :] = bterm[:, sub].reshape(rows // SUBLANES, SUBLANES, HEAD_DIM)

    def scan_part(q, hs):
        per = groups_per_seg // SCAN_PARTS
        hs = list(hs)
        for j in range(q * per, (q + 1) * per):
            for rr in range(SUBLANES):
                for s in range(nseg):
                    jj = s * groups_per_seg + j
                    at = a3[jj, pl.ds(rr, HEADS, stride=SUBLANES), :]
                    bt = b3[jj, pl.ds(rr, HEADS, stride=SUBLANES), :]
                    hs[s] = at * hs[s] + bt
                    h3[jj, pl.ds(rr, HEADS, stride=SUBLANES), :] = hs[s]
        return hs

    def y_lru_pair(p):
        return jnp.concatenate(
            [h3[:, c * SUBLANES:(c + 1) * SUBLANES, :].reshape(rows, HEAD_DIM) for c in (2 * p, 2 * p + 1)],
            axis=1)

    def layernorm_v(vg):
        mu = _rowsum(vg) * (1.0 / gw)
        vc = [g - mu for g in vg]
        rstd = lax.rsqrt(_rowsum([c * c for c in vc]) * (1.0 / gw) + EPS)
        return [vc[p] * rstd * vec(_LNG, pair_cols(p)) + vec(_LNB, pair_cols(p)) for p in range(NPAIR)]

    chunk = min(seg, MLP_CHUNK)
    nchunk = rows // chunk
    tri = (lax.broadcasted_iota(jnp.int32, (MLP_CHUNK, MLP_CHUNK), 0)
           >= lax.broadcasted_iota(jnp.int32, (MLP_CHUNK, MLP_CHUNK), 1))

    def token_mlp(h, vb, gu):
        p, hh = divmod(h, 2)
        sub = slice(hh * HEAD_DIM, (hh + 1) * HEAD_DIM)
        w = jnp.where(tri, wsp_ref[h], 0.0).astype(_BF16)[:chunk, :chunk]
        vh = jnp.concatenate([vb[p][k * chunk:(k + 1) * chunk, sub] for k in range(nchunk)], axis=1)
        m = _dot(w, vh)
        bias = bsb_ref[h][:chunk]
        mixed = jnp.concatenate(
            [m[:, k * HEAD_DIM:(k + 1) * HEAD_DIM] + bias for k in range(nchunk)], axis=0)
        return gu[p][:, sub] * mixed

    xa0 = proj(0, 0)
    xa1 = proj(0, 1)
    xc0 = conv(0, xa0)
    xa2 = proj(0, 2)
    xc1 = conv(1, xa1)
    gt0 = gates(0, xc0)
    xa3 = proj(0, 3)
    xc2 = conv(2, xa2)
    gt1 = gates(1, xc1)
    u = [None] * NPAIR
    u[0] = proj(2, 0)
    xc3 = conv(3, xa3)
    gt2 = gates(2, xc2)
    u[1] = proj(2, 1)
    lru_coeffs(0, xc0, *gt0)
    gt3 = gates(3, xc3)
    u[2] = proj(2, 2)
    lru_coeffs(1, xc1, *gt1)
    u[3] = proj(2, 3)
    lru_coeffs(2, xc2, *gt2)
    v = [None] * NPAIR
    v[0] = proj(3, 0)
    lru_coeffs(3, xc3, *gt3)

    hs = [hcar[...] if carry else h0_ref[s] for s in range(nseg)]
    gu = [None] * NPAIR
    vg = [None] * NPAIR
    v[1] = proj(3, 1)
    hs = scan_part(0, hs)
    gu[0] = _gelu(u[0])
    gu[1] = _gelu(u[1])
    v[2] = proj(3, 2)
    for src, dst in zip(cast_src, cast_dst):
        _convert_slab(src, dst)
    hs = scan_part(1, hs)
    gu[2] = _gelu(u[2])
    gu[3] = _gelu(u[3])
    v[3] = proj(3, 3)
    hs = scan_part(2, hs)
    vg[0] = _gelu(v[0])
    vg[1] = _gelu(v[1])
    ga = [None] * NPAIR
    ga[0] = proj(1, 0)
    hs = scan_part(3, hs)
    if carry:
        hcar[...] = hs[0]
    for s in range(nseg):
        ho_ref[s] = hs[s]
    vg[2] = _gelu(v[2])
    vg[3] = _gelu(v[3])

    ga[1] = proj(1, 1)
    v_n = layernorm_v(vg)
    if emit_vrows:
        for p in range(NPAIR):
            vrows_ref[:, pair_cols(p)] = v_n[p]
    vb = [n.astype(_BF16) for n in v_n]
    ga[2] = proj(1, 2)
    gga = [_gelu(ga[0]), _gelu(ga[1]), None, None]
    out_b = [token_mlp(h, vb, gu) for h in range(HEADS // 2)]
    ga[3] = proj(1, 3)
    gga[2] = _gelu(ga[2])
    out_b += [token_mlp(h, vb, gu) for h in range(HEADS // 2, HEADS)]
    gga[3] = _gelu(ga[3])
    out_a = [y_lru_pair(p) * gga[p] for p in range(NPAIR)]
    rs_a = lax.rsqrt(_rowsum([o * o for o in out_a]) * (1.0 / gw) + EPS)
    na = jnp.concatenate([out_a[p] * rs_a * vec(_GNA, pair_cols(p)) for p in range(NPAIR)],
                         axis=1).astype(_BF16)
    acc = [_dot(na, wout_ref[n, 0:gw, :]) for n in range(OUT_CHUNKS)]
    rs_b = lax.rsqrt(_rowsum([o * o for o in out_b]) * (1.0 / gw) + EPS)
    nb = jnp.concatenate(
        [out_b[h] * rs_b * vec(_GNB, slice(h * HEAD_DIM, (h + 1) * HEAD_DIM)) for h in range(HEADS)],
        axis=1).astype(_BF16)
    for n in range(OUT_CHUNKS):
        cols = slice(n * OUT_COLS, (n + 1) * OUT_COLS)
        y_ref[:, cols] = x[:, cols] + acc[n] + _dot(nb, wout_ref[n, gw:2 * gw, :])


def _layer_spec(shape, layer):
    zeros = (0,) * (len(shape) - 1)
    return pl.BlockSpec((None,) + tuple(shape[1:]), lambda *_: (layer,) + zeros,
                        pipeline_mode=pl.Buffered(1))


def _whole_spec(shape):
    zeros = (0,) * len(shape)
    return pl.BlockSpec(tuple(shape), lambda *_: zeros, pipeline_mode=pl.Buffered(1))


def _mixer(x2d, conv0, h0, norm1, cvec, win, wgate, wsp, bsb, wout, casts=(), *, layer, state_layer,
           seg, nseg, carry, emit_vrows, name):
    rows = seg * nseg
    total = x2d.shape[0]
    nseq = conv0.shape[1]
    gw = GROUP_WIDTH
    if carry:
        steps = total // nseq // rows
        grid = (nseq, steps)
        row_map = lambda b, t: (b * steps + t, 0)
        seq_map = lambda b, t: (b, 0, 0)
        state_map = lambda b, t: (state_layer, b, 0, 0)
        cast_in, cast_out, cast_shape = _convert_specs(casts, nseq * steps, lambda b, t: b * steps + t)
    else:
        assert not casts
        grid = (total // rows,)
        row_map = lambda i: (i, 0)
        seq_map = lambda i: (i, 0, 0)
        state_map = lambda i: (state_layer, i, 0, 0)
        cast_in, cast_out, cast_shape = [], [], []
    out_shape = [jax.ShapeDtypeStruct((total, D_MODEL), _F32),
                 jax.ShapeDtypeStruct((nseq, CARRY_ROWS, gw), _F32),
                 jax.ShapeDtypeStruct((nseq, HEADS, HEAD_DIM), _F32)]
    out_specs = [pl.BlockSpec((rows, D_MODEL), row_map),
                 pl.BlockSpec((nseg, CARRY_ROWS, gw), seq_map),
                 pl.BlockSpec((nseg, HEADS, HEAD_DIM), seq_map)]
    if emit_vrows:
        out_shape.append(jax.ShapeDtypeStruct((total, gw), _F32))
        out_specs.append(pl.BlockSpec((rows, gw), row_map))
    scan_shape = (rows // SUBLANES, HEADS * SUBLANES, HEAD_DIM)
    return pl.pallas_call(
        functools.partial(_mixer_kernel, seg=seg, nseg=nseg, carry=carry, emit_vrows=emit_vrows,
                          ncast=len(casts)),
        grid=grid,
        in_specs=[pl.BlockSpec((rows, D_MODEL), row_map),
                  pl.BlockSpec((None, nseg, CARRY_ROWS, gw), state_map),
                  pl.BlockSpec((None, nseg, HEADS, HEAD_DIM), state_map),
                  _layer_spec(norm1.shape, layer), _layer_spec(cvec.shape, layer),
                  _whole_spec(win.shape), _layer_spec(wgate.shape, layer),
                  _layer_spec(wsp.shape, layer), _layer_spec(bsb.shape, layer),
                  _whole_spec(wout.shape), *cast_in],
        out_specs=out_specs + cast_out,
        out_shape=out_shape + cast_shape,
        scratch_shapes=[pltpu.VMEM((nseg, seg + CARRY_ROWS, gw), _F32),
                        pltpu.VMEM(scan_shape, _F32), pltpu.VMEM(scan_shape, _F32),
                        pltpu.VMEM(scan_shape, _F32), pltpu.VMEM((HEADS, HEAD_DIM), _F32)],
        compiler_params=pltpu.CompilerParams(
            dimension_semantics=("arbitrary",) * len(grid), vmem_limit_bytes=VMEM_LIMIT),
        name=name,
    )(x2d, conv0, h0, norm1, cvec, win, wgate, wsp, bsb, wout, *[w for w, _, _ in casts])


def _ffn_kernel(h_ref, norm2_ref, wg_ref, wu_ref, *rest, final):
    wd_refs, (normf_ref, o_ref, hn_ref) = rest[:OUT_CHUNKS], rest[OUT_CHUNKS:]
    j = pl.program_id(1)

    @pl.when(j == 0)
    def _():
        h = h_ref[...]
        hn_ref[...] = _rms(h, norm2_ref[...]).astype(_BF16)
        o_ref[...] = h

    hn = hn_ref[...]
    g = _dot(hn, wg_ref[...])
    u = _dot(hn, wu_ref[...])
    act = ((g * _sigmoid(g)) * u).astype(_BF16)
    for n in range(OUT_CHUNKS):
        cols = slice(n * OUT_COLS, (n + 1) * OUT_COLS)
        o_ref[:, cols] += _dot(act, wd_refs[n][...])

    if final:
        @pl.when(j == pl.num_programs(1) - 1)
        def _():
            o_ref[...] = _rms(o_ref[...], normf_ref[...])


def _ffn(h2d, norm2, wg, wu, wd, normf, *, layer, final, name):
    total = h2d.shape[0]
    d_ff = wg.shape[-1]
    rows = min(FFN_ROWS, total)
    grid = (total // rows, d_ff // FFN_COLS)
    return pl.pallas_call(
        functools.partial(_ffn_kernel, final=final),
        grid=grid,
        in_specs=[pl.BlockSpec((rows, D_MODEL), lambda i, j: (i, 0)),
                  _layer_spec(norm2.shape, layer),
                  pl.BlockSpec((D_MODEL, FFN_COLS), lambda i, j: (0, j)),
                  pl.BlockSpec((D_MODEL, FFN_COLS), lambda i, j: (0, j)),
                  *[pl.BlockSpec((FFN_COLS, OUT_COLS), lambda i, j, n=n: (j, n)) for n in range(OUT_CHUNKS)],
                  _layer_spec(normf.shape, 0)],
        out_specs=pl.BlockSpec((rows, D_MODEL), lambda i, j: (i, 0)),
        out_shape=jax.ShapeDtypeStruct((total, D_MODEL), _F32),
        scratch_shapes=[pltpu.VMEM((rows, D_MODEL), _BF16)],
        compiler_params=pltpu.CompilerParams(
            dimension_semantics=("arbitrary", "arbitrary"), vmem_limit_bytes=VMEM_LIMIT),
        name=name,
    )(h2d, norm2, wg, wu, *([wd] * OUT_CHUNKS), normf)


def kernel(x_prompt, x_sample, state_conv, state_lru, norm1, w_in, conv_w, conv_b, w_rgate, b_rgate,
           w_igate, b_igate, lru_param, v_ln_g, v_ln_b, w_spatial, b_spatial, gn_a, gn_b, w_out,
           norm2, w_gate, w_up, w_down, norm_f):
    depth = w_in.shape[0]
    batch, seq, _ = x_prompt.shape
    dec_batch, dec_seq, _ = x_sample.shape
    gw = GROUP_WIDTH

    xp = x_prompt.reshape(batch * seq, D_MODEL)
    xs = x_sample.reshape(dec_batch * dec_seq, D_MODEL)
    conv0_p = jnp.zeros((1, batch, CARRY_ROWS, gw), _F32)
    h0_p = jnp.zeros((1, batch, HEADS, HEAD_DIM), _F32)
    conv0_s = jnp.pad(state_conv, ((0, 0), (0, 0), (CARRY_ROWS - (CONV_WIDTH - 1), 0), (0, 0)))
    h0_s = state_lru.reshape(depth, dec_batch, HEADS, HEAD_DIM)

    row = lambda a: a[:, None, :]
    cvec = jnp.concatenate(
        [conv_w, row(conv_b), row(b_rgate), row(b_igate), row(lru_param), row(v_ln_g), row(v_ln_b),
         row(gn_a), row(gn_b), jnp.zeros((depth, _CVEC_ROWS - 12, gw), _F32)], axis=1)
    wgate = jnp.concatenate([w_rgate, w_igate], axis=-1).astype(_BF16)
    bsb = jnp.broadcast_to(b_spatial[..., None], (depth, HEADS, MLP_CHUNK, HEAD_DIM))
    normf = norm_f.reshape(1, 1, D_MODEL)
    steps = seq // MIXER_ROWS

    def proj_casts(l):
        return [(w_in, l, PAIR), (w_out, l, OUT_COLS)]

    win, wout = _convert(proj_casts(0), steps, name="convert_proj_0")

    conv_p, lru_p, conv_s, lru_s, vrows_s = [], [], [], [], []
    for l in range(depth):
        final = l == depth - 1
        mixer_w = (norm1[:, None, :], cvec, win, wgate, w_spatial, bsb, wout)
        casts = [(w_gate, l, None), (w_up, l, None), (w_down, l, None)]
        if not final:
            casts += proj_casts(l + 1)
        hp, cp, lp, wg, wu, wd, *nxt = _mixer(xp, conv0_p, h0_p, *mixer_w, casts, layer=l,
                                              state_layer=0, seg=MIXER_ROWS, nseg=1, carry=True,
                                              emit_vrows=False, name=f"mixer_prompt_{l}")
        ffn_w = (norm2[:, None, :], wg, wu, wd, normf)
        hs, cs, ls, vs = _mixer(xs, conv0_s, h0_s, *mixer_w, layer=l, state_layer=l, seg=dec_seq,
                                nseg=MIXER_ROWS // dec_seq, carry=False, emit_vrows=True,
                                name=f"mixer_sample_{l}")
        xp = _ffn(hp, *ffn_w, layer=l, final=final, name=f"ffn_prompt_{l}")
        xs = _ffn(hs, *ffn_w, layer=l, final=final, name=f"ffn_sample_{l}")
        if not final:
            win, wout = nxt

        keep = slice(CARRY_ROWS - (CONV_WIDTH - 1), CARRY_ROWS)
        conv_p.append(cp[:, keep])
        lru_p.append(lp.reshape(batch, gw))
        conv_s.append(cs[:, keep])
        lru_s.append(ls.reshape(dec_batch, gw))
        vrows_s.append(vs.reshape(dec_batch, dec_seq, gw))

    return (xp.reshape(batch, seq, D_MODEL), xs.reshape(dec_batch, dec_seq, D_MODEL),
            jnp.stack(conv_p), jnp.stack(lru_p), jnp.stack(conv_s), jnp.stack(lru_s),
            jnp.stack(vrows_s))
```

```python
import functools
import math

import jax
import jax.numpy as jnp
from jax import lax
from jax.experimental import pallas as pl
from jax.experimental.pallas import tpu as pltpu

D_MODEL = 2048
GROUP_WIDTH = D_MODEL // 2
HEADS = 8
HEAD_DIM = GROUP_WIDTH // HEADS
CONV_WIDTH = 4
MLP_CHUNK = 128
LRU_C = 8.0
EPS = 1e-6

SUBLANES = 8
BF16_ROWS = 2 * SUBLANES
CARRY_ROWS = SUBLANES
CHUNK_HEADS = 2
PAIR = CHUNK_HEADS * HEAD_DIM
NPAIR = GROUP_WIDTH // PAIR
SCAN_PARTS = 4
OUT_COLS = 512
OUT_CHUNKS = D_MODEL // OUT_COLS

MIXER_ROWS = 256
FFN_ROWS = 1024
FFN_COLS = 512
VMEM_LIMIT = 60 * 1024 * 1024

_CW0, _CB, _BR, _BI, _LAM, _LNG, _LNB, _GNA, _GNB = 0, 4, 5, 6, 7, 8, 9, 10, 11
_CVEC_ROWS = 16

_BF16 = jnp.bfloat16
_F32 = jnp.float32
_MIN_NORMAL = float(jnp.finfo(jnp.float32).tiny)


def _dot(a, b):
    return jnp.dot(a, b, preferred_element_type=_F32)


def _rms(x, g):
    return x * lax.rsqrt(jnp.mean(x * x, axis=-1, keepdims=True) + EPS) * g


def _gelu(x):
    c = math.sqrt(2.0 / math.pi)
    hx = 0.5 * x
    return hx + hx * jnp.tanh(x * (c + (c * 0.044715) * (x * x)))


def _sigmoid(x):
    return 1.0 / (1.0 + jnp.exp(-x))


def _rowsum(parts):
    total = jnp.sum(parts[0], axis=-1, keepdims=True)
    for p in parts[1:]:
        total = total + jnp.sum(p, axis=-1, keepdims=True)
    return total


def _convert_slab(src, dst):
    if len(dst.shape) == 2:
        dst[...] = src[...].astype(_BF16)
    else:
        width = dst.shape[2]
        for c in range(dst.shape[0]):
            dst[c] = src[:, c * width:(c + 1) * width].astype(_BF16)


def _convert_specs(casts, steps, step_of):
    in_specs, out_specs, out_shape = [], [], []
    for w, layer, width in casts:
        _, nrow, ncol = w.shape
        hold = next(h for h in (1, 2, 4, 8)
                    if nrow * h % steps == 0 and (nrow * h // steps) % BF16_ROWS == 0)
        slab = nrow * hold // steps
        index = lambda *g, hold=hold: step_of(*g) // hold
        in_specs.append(pl.BlockSpec((None, slab, ncol),
                                     lambda *g, layer=layer, index=index: (layer, index(*g), 0)))
        if width is None:
            out_shape.append(jax.ShapeDtypeStruct((nrow, ncol), _BF16))
            out_specs.append(pl.BlockSpec((slab, ncol), lambda *g, index=index: (index(*g), 0)))
        else:
            out_shape.append(jax.ShapeDtypeStruct((ncol // width, nrow, width), _BF16))
            out_specs.append(pl.BlockSpec((ncol // width, slab, width),
                                          lambda *g, index=index: (0, index(*g), 0)))
    return in_specs, out_specs, out_shape


def _tril_kernel(w_ref, o_ref):
    tri = (lax.broadcasted_iota(jnp.int32, w_ref.shape, 1)
           >= lax.broadcasted_iota(jnp.int32, w_ref.shape, 2))
    o_ref[...] = jnp.where(tri, w_ref[...], 0.0).astype(_BF16)


def _tril(w_spatial):
    depth = w_spatial.shape[0]
    block = (None,) + tuple(w_spatial.shape[1:])
    return pl.pallas_call(
        _tril_kernel, grid=(depth,),
        in_specs=[pl.BlockSpec(block, lambda l: (l, 0, 0, 0))],
        out_specs=pl.BlockSpec(block, lambda l: (l, 0, 0, 0)),
        out_shape=jax.ShapeDtypeStruct(w_spatial.shape, _BF16),
        name="tril_spatial",
    )(w_spatial)


def _convert_kernel(*refs):
    half = len(refs) // 2
    for src, dst in zip(refs[:half], refs[half:]):
        _convert_slab(src, dst)


def _convert(casts, steps, name):
    in_specs, out_specs, out_shape = _convert_specs(casts, steps, lambda i: i)
    return pl.pallas_call(
        _convert_kernel, grid=(steps,), in_specs=in_specs, out_specs=out_specs, out_shape=out_shape,
        compiler_params=pltpu.CompilerParams(
            dimension_semantics=("arbitrary",), vmem_limit_bytes=VMEM_LIMIT),
        name=name,
    )(*[w for w, _, _ in casts])


def _mixer_kernel(*refs, seg, nseg, carry, emit_vrows, ncast):
    it = iter(refs)
    take = lambda n: [next(it) for _ in range(n)]
    x_ref, conv0_ref, h0_ref, norm1_ref, cvec_ref = take(5)
    win_ref, wgate_ref, wsp_ref, bsb_ref, wout_ref = take(5)
    cast_src = take(ncast)
    y_ref, convo_ref, ho_ref = take(3)
    vrows_ref = take(1)[0] if emit_vrows else None
    cast_dst = take(ncast)
    xpad, a3, b3, h3, hcar = take(5)
    rows = seg * nseg
    gw = GROUP_WIDTH
    groups_per_seg = seg // SUBLANES

    def vec(k, cols=slice(None)):
        return cvec_ref[k:k + 1, cols]

    def pair_cols(p):
        return slice(p * PAIR, (p + 1) * PAIR)

    if carry:
        t = pl.program_id(1)

        @pl.when(t == 0)
        def _():
            xpad[0, 0:CARRY_ROWS, :] = conv0_ref[0]
            hcar[...] = h0_ref[0]

        @pl.when(t > 0)
        def _():
            xpad[0, 0:CARRY_ROWS, :] = xpad[0, seg:seg + CARRY_ROWS, :]
    else:
        for s in range(nseg):
            xpad[s, 0:CARRY_ROWS, :] = conv0_ref[s]

    x = x_ref[...]
    xnb = _rms(x, norm1_ref[...]).astype(_BF16)

    def proj(group, p):
        return _dot(xnb, win_ref[group * NPAIR + p])

    def conv(p, xa_p):
        cols = pair_cols(p)
        for s in range(nseg):
            xpad[s, CARRY_ROWS:CARRY_ROWS + seg, cols] = xa_p[s * seg:(s + 1) * seg]
            convo_ref[s, :, cols] = xa_p[(s + 1) * seg - CARRY_ROWS:(s + 1) * seg]

        def shifted(k):
            parts = [xpad[s, CARRY_ROWS - k:CARRY_ROWS - k + seg, cols] for s in range(nseg)]
            return parts[0] if nseg == 1 else jnp.concatenate(parts, axis=0)

        return (shifted(3) * vec(_CW0, cols) + shifted(2) * vec(_CW0 + 1, cols)
                + shifted(1) * vec(_CW0 + 2, cols) + xa_p * vec(_CW0 + 3, cols) + vec(_CB, cols))

    def gates(p, xc_p):
        g = [_dot(xc_p[:, hh * HEAD_DIM:(hh + 1) * HEAD_DIM].astype(_BF16), wgate_ref[CHUNK_HEADS * p + hh])
             for hh in range(CHUNK_HEADS)]
        return (jnp.concatenate([gh[:, :HEAD_DIM] for gh in g], axis=1),
                jnp.concatenate([gh[:, HEAD_DIM:] for gh in g], axis=1))

    neg_lam = -vec(_LAM)
    softplus = jnp.maximum(neg_lam, 0.0) + jnp.log1p(jnp.exp(-jnp.abs(neg_lam)))
    neg_c_softplus = -LRU_C * softplus

    def lru_coeffs(p, xc_p, g_r, g_i):
        cols = pair_cols(p)
        r = _sigmoid(g_r + vec(_BR, cols))
        ig = _sigmoid(g_i + vec(_BI, cols))
        log_a = r * neg_c_softplus[:, cols]
        a = jnp.exp(log_a)
        m = -jnp.tanh(log_a) * (a * a + 1.0)
        bterm = (m * lax.rsqrt(jnp.maximum(m, _MIN_NORMAL))) * (ig * xc_p)
        for hh in range(CHUNK_HEADS):
            c = CHUNK_HEADS * p + hh
            sub = slice(hh * HEAD_DIM, (hh + 1) * HEAD_DIM)
            a3[:, c * SUBLANES:(c + 1) * SUBLANES, :] = a[:, sub].reshape(rows // SUBLANES, SUBLANES, HEAD_DIM)
            b3[:, c * SUBLANES:(c + 1) * SUBLANES, :] = bterm[:, sub].reshape(rows // SUBLANES, SUBLANES, HEAD_DIM)

    def scan_part(q, hs):
        per = groups_per_seg // SCAN_PARTS
        hs = list(hs)
        for j in range(q * per, (q + 1) * per):
            for rr in range(SUBLANES):
                for s in range(nseg):
                    jj = s * groups_per_seg + j
                    at = a3[jj, pl.ds(rr, HEADS, stride=SUBLANES), :]
                    bt = b3[jj, pl.ds(rr, HEADS, stride=SUBLANES), :]
                    hs[s] = at * hs[s] + bt
                    h3[jj, pl.ds(rr, HEADS, stride=SUBLANES), :] = hs[s]
        return hs

    def y_lru_pair(p):
        return jnp.concatenate(
            [h3[:, c * SUBLANES:(c + 1) * SUBLANES, :].reshape(rows, HEAD_DIM)
             for c in range(CHUNK_HEADS * p, CHUNK_HEADS * (p + 1))],
            axis=1)

    def layernorm_v(vg):
        mu = _rowsum(vg) * (1.0 / gw)
        vc = [g - mu for g in vg]
        rstd = lax.rsqrt(_rowsum([c * c for c in vc]) * (1.0 / gw) + EPS)
        return [vc[p] * rstd * vec(_LNG, pair_cols(p)) + vec(_LNB, pair_cols(p)) for p in range(NPAIR)]

    chunk = min(seg, MLP_CHUNK)
    nchunk = rows // chunk

    def token_mlp(h, vb, gu):
        p, hh = divmod(h, CHUNK_HEADS)
        sub = slice(hh * HEAD_DIM, (hh + 1) * HEAD_DIM)
        vh = jnp.concatenate([vb[p][k * chunk:(k + 1) * chunk, sub] for k in range(nchunk)], axis=1)
        m = _dot(wsp_ref[h, 0:chunk, 0:chunk], vh)
        bias = bsb_ref[h][:chunk]
        mixed = jnp.concatenate(
            [m[:, k * HEAD_DIM:(k + 1) * HEAD_DIM] + bias for k in range(nchunk)], axis=0)
        return gu[p][:, sub] * mixed

    def convert_weights():
        for src, dst in zip(cast_src, cast_dst):
            _convert_slab(src, dst)

    xa0 = proj(0, 0)
    xa1 = proj(0, 1)
    xc0 = conv(0, xa0)
    xa2 = proj(0, 2)
    xc1 = conv(1, xa1)
    gt0 = gates(0, xc0)
    xa3 = proj(0, 3)
    xc2 = conv(2, xa2)
    gt1 = gates(1, xc1)
    v = [None] * NPAIR
    v[0] = proj(3, 0)
    xc3 = conv(3, xa3)
    gt2 = gates(2, xc2)
    v[1] = proj(3, 1)
    lru_coeffs(0, xc0, *gt0)
    gt3 = gates(3, xc3)
    v[2] = proj(3, 2)
    lru_coeffs(1, xc1, *gt1)
    v[3] = proj(3, 3)
    lru_coeffs(2, xc2, *gt2)
    u = [None] * NPAIR
    u[0] = proj(2, 0)
    lru_coeffs(3, xc3, *gt3)

    hs = [hcar[...] if carry else h0_ref[s] for s in range(nseg)]
    u[1] = proj(2, 1)
    hs = scan_part(0, hs)
    vg = [_gelu(v[0]), _gelu(v[1])]
    u[2] = proj(2, 2)
    convert_weights()
    hs = scan_part(1, hs)
    vg += [_gelu(v[2]), _gelu(v[3])]
    u[3] = proj(2, 3)
    hs = scan_part(2, hs)
    v_n = layernorm_v(vg)
    if emit_vrows:
        for p in range(NPAIR):
            vrows_ref[:, pair_cols(p)] = v_n[p]
    vb = [n.astype(_BF16) for n in v_n]
    ga = [None] * NPAIR
    ga[0] = proj(1, 0)
    hs = scan_part(3, hs)
    if carry:
        hcar[...] = hs[0]
    for s in range(nseg):
        ho_ref[s] = hs[s]
    gu = [_gelu(u[p]) for p in range(NPAIR)]
    ga[1] = proj(1, 1)
    out_b = [token_mlp(h, vb, gu) for h in range(HEADS // 2)]
    ga[2] = proj(1, 2)
    out_b += [token_mlp(h, vb, gu) for h in range(HEADS // 2, HEADS)]
    gga = [_gelu(ga[0]), _gelu(ga[1])]
    ga[3] = proj(1, 3)
    rs_b = lax.rsqrt(_rowsum([o * o for o in out_b]) * (1.0 / gw) + EPS)
    nb = jnp.concatenate(
        [out_b[h] * rs_b * vec(_GNB, slice(h * HEAD_DIM, (h + 1) * HEAD_DIM)) for h in range(HEADS)],
        axis=1).astype(_BF16)
    gga.append(_gelu(ga[2]))

    acc = [_dot(nb, wout_ref[n, gw:2 * gw, :]) for n in range(OUT_CHUNKS)]
    gga.append(_gelu(ga[3]))
    out_a = [y_lru_pair(p) * gga[p] for p in range(NPAIR)]
    rs_a = lax.rsqrt(_rowsum([o * o for o in out_a]) * (1.0 / gw) + EPS)
    na = jnp.concatenate([out_a[p] * rs_a * vec(_GNA, pair_cols(p)) for p in range(NPAIR)],
                         axis=1).astype(_BF16)
    for n in range(OUT_CHUNKS):
        cols = slice(n * OUT_COLS, (n + 1) * OUT_COLS)
        y_ref[:, cols] = x[:, cols] + acc[n] + _dot(na, wout_ref[n, 0:gw, :])


def _layer_spec(shape, layer):
    zeros = (0,) * (len(shape) - 1)
    return pl.BlockSpec((None,) + tuple(shape[1:]), lambda *_: (layer,) + zeros,
                        pipeline_mode=pl.Buffered(1))


def _whole_spec(shape):
    zeros = (0,) * len(shape)
    return pl.BlockSpec(tuple(shape), lambda *_: zeros, pipeline_mode=pl.Buffered(1))


def _mixer(x2d, conv0, h0, norm1, cvec, win, wgate, wsp, bsb, wout, casts=(), *, layer, state_layer,
           seg, nseg, carry, emit_vrows, name):
    rows = seg * nseg
    total = x2d.shape[0]
    nseq = conv0.shape[1]
    gw = GROUP_WIDTH
    if carry:
        steps = total // nseq // rows
        grid = (nseq, steps)
        row_map = lambda b, t: (b * steps + t, 0)
        seq_map = lambda b, t: (b, 0, 0)
        state_map = lambda b, t: (state_layer, b, 0, 0)
        cast_in, cast_out, cast_shape = _convert_specs(casts, nseq * steps, lambda b, t: b * steps + t)
    else:
        assert not casts
        grid = (total // rows,)
        row_map = lambda i: (i, 0)
        seq_map = lambda i: (i, 0, 0)
        state_map = lambda i: (state_layer, i, 0, 0)
        cast_in, cast_out, cast_shape = [], [], []
    out_shape = [jax.ShapeDtypeStruct((total, D_MODEL), _F32),
                 jax.ShapeDtypeStruct((nseq, CARRY_ROWS, gw), _F32),
                 jax.ShapeDtypeStruct((nseq, HEADS, HEAD_DIM), _F32)]
    out_specs = [pl.BlockSpec((rows, D_MODEL), row_map),
                 pl.BlockSpec((nseg, CARRY_ROWS, gw), seq_map),
                 pl.BlockSpec((nseg, HEADS, HEAD_DIM), seq_map)]
    if emit_vrows:
        out_shape.append(jax.ShapeDtypeStruct((total, gw), _F32))
        out_specs.append(pl.BlockSpec((rows, gw), row_map))
    scan_shape = (rows // SUBLANES, HEADS * SUBLANES, HEAD_DIM)
    return pl.pallas_call(
        functools.partial(_mixer_kernel, seg=seg, nseg=nseg, carry=carry, emit_vrows=emit_vrows,
                          ncast=len(casts)),
        grid=grid,
        in_specs=[pl.BlockSpec((rows, D_MODEL), row_map),
                  pl.BlockSpec((None, nseg, CARRY_ROWS, gw), state_map),
                  pl.BlockSpec((None, nseg, HEADS, HEAD_DIM), state_map),
                  _layer_spec(norm1.shape, layer), _layer_spec(cvec.shape, layer),
                  _whole_spec(win.shape), _layer_spec(wgate.shape, layer),
                  _layer_spec(wsp.shape, layer), _layer_spec(bsb.shape, layer),
                  _whole_spec(wout.shape), *cast_in],
        out_specs=out_specs + cast_out,
        out_shape=out_shape + cast_shape,
        scratch_shapes=[pltpu.VMEM((nseg, seg + CARRY_ROWS, gw), _F32),
                        pltpu.VMEM(scan_shape, _F32), pltpu.VMEM(scan_shape, _F32),
                        pltpu.VMEM(scan_shape, _F32), pltpu.VMEM((HEADS, HEAD_DIM), _F32)],
        compiler_params=pltpu.CompilerParams(
            dimension_semantics=("arbitrary",) * len(grid), vmem_limit_bytes=VMEM_LIMIT),
        name=name,
    )(x2d, conv0, h0, norm1, cvec, win, wgate, wsp, bsb, wout, *[w for w, _, _ in casts])


def _ffn_kernel(h_ref, norm2_ref, wg_ref, wu_ref, *rest, final):
    wd_refs, (normf_ref, o_ref, hn_ref) = rest[:OUT_CHUNKS], rest[OUT_CHUNKS:]
    j = pl.program_id(1)

    @pl.when(j == 0)
    def _():
        h = h_ref[...]
        hn_ref[...] = _rms(h, norm2_ref[...]).astype(_BF16)
        o_ref[...] = h

    hn = hn_ref[...]
    g = _dot(hn, wg_ref[...])
    u = _dot(hn, wu_ref[...])
    act = ((g * _sigmoid(g)) * u).astype(_BF16)
    for n in range(OUT_CHUNKS):
        cols = slice(n * OUT_COLS, (n + 1) * OUT_COLS)
        o_ref[:, cols] += _dot(act, wd_refs[n][...])

    if final:
        @pl.when(j == pl.num_programs(1) - 1)
        def _():
            o_ref[...] = _rms(o_ref[...], normf_ref[...])


def _ffn(h2d, norm2, wg, wu, wd, normf, *, layer, final, name):
    total = h2d.shape[0]
    d_ff = wg.shape[-1]
    rows = min(FFN_ROWS, total)
    grid = (total // rows, d_ff // FFN_COLS)
    return pl.pallas_call(
        functools.partial(_ffn_kernel, final=final),
        grid=grid,
        in_specs=[pl.BlockSpec((rows, D_MODEL), lambda i, j: (i, 0)),
                  _layer_spec(norm2.shape, layer),
                  pl.BlockSpec((D_MODEL, FFN_COLS), lambda i, j: (0, j)),
                  pl.BlockSpec((D_MODEL, FFN_COLS), lambda i, j: (0, j)),
                  *[pl.BlockSpec((FFN_COLS, OUT_COLS), lambda i, j, n=n: (j, n)) for n in range(OUT_CHUNKS)],
                  _layer_spec(normf.shape, 0)],
        out_specs=pl.BlockSpec((rows, D_MODEL), lambda i, j: (i, 0)),
        out_shape=jax.ShapeDtypeStruct((total, D_MODEL), _F32),
        scratch_shapes=[pltpu.VMEM((rows, D_MODEL), _BF16)],
        compiler_params=pltpu.CompilerParams(
            dimension_semantics=("arbitrary", "arbitrary"), vmem_limit_bytes=VMEM_LIMIT),
        name=name,
    )(h2d, norm2, wg, wu, *([wd] * OUT_CHUNKS), normf)


def kernel(x_prompt, x_sample, state_conv, state_lru, norm1, w_in, conv_w, conv_b, w_rgate, b_rgate,
           w_igate, b_igate, lru_param, v_ln_g, v_ln_b, w_spatial, b_spatial, gn_a, gn_b, w_out,
           norm2, w_gate, w_up, w_down, norm_f):
    depth = w_in.shape[0]
    batch, seq, _ = x_prompt.shape
    dec_batch, dec_seq, _ = x_sample.shape
    gw = GROUP_WIDTH

    xp = x_prompt.reshape(batch * seq, D_MODEL)
    xs = x_sample.reshape(dec_batch * dec_seq, D_MODEL)
    conv0_p = jnp.zeros((1, batch, CARRY_ROWS, gw), _F32)
    h0_p = jnp.zeros((1, batch, HEADS, HEAD_DIM), _F32)
    conv0_s = jnp.pad(state_conv, ((0, 0), (0, 0), (CARRY_ROWS - (CONV_WIDTH - 1), 0), (0, 0)))
    h0_s = state_lru.reshape(depth, dec_batch, HEADS, HEAD_DIM)

    row = lambda a: a[:, None, :]
    cvec = jnp.concatenate(
        [conv_w, row(conv_b), row(b_rgate), row(b_igate), row(lru_param), row(v_ln_g), row(v_ln_b),
         row(gn_a), row(gn_b), jnp.zeros((depth, _CVEC_ROWS - 12, gw), _F32)], axis=1)
    wgate = jnp.concatenate([w_rgate, w_igate], axis=-1).astype(_BF16)
    bsb = jnp.broadcast_to(b_spatial[..., None], (depth, HEADS, MLP_CHUNK, HEAD_DIM))
    normf = norm_f.reshape(1, 1, D_MODEL)
    steps = seq // MIXER_ROWS

    def proj_casts(l):
        return [(w_in, l, PAIR), (w_out, l, OUT_COLS)]

    win, wout = _convert(proj_casts(0), steps, name="convert_proj_0")
    wsp = _tril(w_spatial)

    conv_p, lru_p, conv_s, lru_s, vrows_s = [], [], [], [], []
    for l in range(depth):
        final = l == depth - 1
        mixer_w = (norm1[:, None, :], cvec, win, wgate, wsp, bsb, wout)
        casts = [(w_gate, l, None), (w_up, l, None), (w_down, l, None)]
        if not final:
            casts += proj_casts(l + 1)
        hp, cp, lp, wg, wu, wd, *nxt = _mixer(xp, conv0_p, h0_p, *mixer_w, casts, layer=l,
                                              state_layer=0, seg=MIXER_ROWS, nseg=1, carry=True,
                                              emit_vrows=False, name=f"mixer_prompt_{l}")
        ffn_w = (norm2[:, None, :], wg, wu, wd, normf)
        hs, cs, ls, vs = _mixer(xs, conv0_s, h0_s, *mixer_w, layer=l, state_layer=l, seg=dec_seq,
                                nseg=MIXER_ROWS // dec_seq, carry=False, emit_vrows=True,
                                name=f"mixer_sample_{l}")
        xp = _ffn(hp, *ffn_w, layer=l, final=final, name=f"ffn_prompt_{l}")
        xs = _ffn(hs, *ffn_w, layer=l, final=final, name=f"ffn_sample_{l}")
        if not final:
            win, wout = nxt

        keep = slice(CARRY_ROWS - (CONV_WIDTH - 1), CARRY_ROWS)
        conv_p.append(cp[:, keep])
        lru_p.append(lp.reshape(batch, gw))
        conv_s.append(cs[:, keep])
        lru_s.append(ls.reshape(dec_batch, gw))
        vrows_s.append(vs.reshape(dec_batch, dec_seq, gw))

    return (xp.reshape(batch, seq, D_MODEL), xs.reshape(dec_batch, dec_seq, D_MODEL),
            jnp.stack(conv_p), jnp.stack(lru_p), jnp.stack(conv_s), jnp.stack(lru_s),
            jnp.stack(vrows_s))
```

```python
import functools
import math

import jax
import jax.numpy as jnp
from jax import lax
from jax.experimental import pallas as pl
from jax.experimental.pallas import tpu as pltpu

D_MODEL = 2048
GROUP_WIDTH = D_MODEL // 2
HEADS = 8
HEAD_DIM = GROUP_WIDTH // HEADS
CONV_WIDTH = 4
MLP_CHUNK = 128
LRU_C = 8.0
EPS = 1e-6

SUBLANES = 8
BF16_ROWS = 2 * SUBLANES
CARRY_ROWS = SUBLANES
CONV_STATE = CONV_WIDTH - 1
CHUNK_HEADS = 2
PAIR = CHUNK_HEADS * HEAD_DIM
NPAIR = GROUP_WIDTH // PAIR
SCAN_PARTS = 4
OUT_COLS = 512
OUT_CHUNKS = D_MODEL // OUT_COLS

MIXER_ROWS = 256
FFN_ROWS = 1024
FFN_COLS = 512
VMEM_LIMIT = 60 * 1024 * 1024
CONVERT_STEPS = 8

_CW0, _CB, _BR, _BI, _LAM, _LNG, _LNB, _GNA, _GNB = 0, 4, 5, 6, 7, 8, 9, 10, 11
_CVEC_ROWS = 16

_BF16 = jnp.bfloat16
_F32 = jnp.float32
_MIN_NORMAL = float(jnp.finfo(jnp.float32).tiny)


def _dot(a, b):
    return jnp.dot(a, b, preferred_element_type=_F32)


def _rms(x, g):
    return x * lax.rsqrt(jnp.mean(x * x, axis=-1, keepdims=True) + EPS) * g


def _gelu(x):
    c = math.sqrt(2.0 / math.pi)
    hx = 0.5 * x
    return hx + hx * jnp.tanh(x * (c + (c * 0.044715) * (x * x)))


def _sigmoid(x):
    return 1.0 / (1.0 + jnp.exp(-x))


def _rowsum(parts):
    total = jnp.sum(parts[0], axis=-1, keepdims=True)
    for p in parts[1:]:
        total = total + jnp.sum(p, axis=-1, keepdims=True)
    return total


def _convert_slab(src, dst):
    if len(dst.shape) == 2:
        dst[...] = src[...].astype(_BF16)
    else:
        width = dst.shape[2]
        for c in range(dst.shape[0]):
            dst[c] = src[:, c * width:(c + 1) * width].astype(_BF16)


def _convert_specs(casts, steps, step_of):
    in_specs, out_specs, out_shape = [], [], []
    for w, layer, width in casts:
        _, nrow, ncol = w.shape
        hold = next(h for h in (1, 2, 4, 8)
                    if nrow * h % steps == 0 and (nrow * h // steps) % BF16_ROWS == 0)
        slab = nrow * hold // steps
        index = lambda *g, hold=hold: step_of(*g) // hold
        in_specs.append(pl.BlockSpec((None, slab, ncol),
                                     lambda *g, layer=layer, index=index: (layer, index(*g), 0)))
        if width is None:
            out_shape.append(jax.ShapeDtypeStruct((nrow, ncol), _BF16))
            out_specs.append(pl.BlockSpec((slab, ncol), lambda *g, index=index: (index(*g), 0)))
        else:
            out_shape.append(jax.ShapeDtypeStruct((ncol // width, nrow, width), _BF16))
            out_specs.append(pl.BlockSpec((ncol // width, slab, width),
                                          lambda *g, index=index: (0, index(*g), 0)))
    return in_specs, out_specs, out_shape


def _tril_kernel(w_ref, o_ref):
    tri = (lax.broadcasted_iota(jnp.int32, w_ref.shape, 1)
           >= lax.broadcasted_iota(jnp.int32, w_ref.shape, 2))
    o_ref[...] = jnp.where(tri, w_ref[...], 0.0).astype(_BF16)


def _tril(w_spatial):
    depth = w_spatial.shape[0]
    block = (None,) + tuple(w_spatial.shape[1:])
    return pl.pallas_call(
        _tril_kernel, grid=(depth,),
        in_specs=[pl.BlockSpec(block, lambda l: (l, 0, 0, 0))],
        out_specs=pl.BlockSpec(block, lambda l: (l, 0, 0, 0)),
        out_shape=jax.ShapeDtypeStruct(w_spatial.shape, _BF16),
        name="tril_spatial",
    )(w_spatial)


def _convert_kernel(*refs):
    half = len(refs) // 2
    for src, dst in zip(refs[:half], refs[half:]):
        _convert_slab(src, dst)


def _convert(casts, steps, name):
    in_specs, out_specs, out_shape = _convert_specs(casts, steps, lambda i: i)
    return pl.pallas_call(
        _convert_kernel, grid=(steps,), in_specs=in_specs, out_specs=out_specs, out_shape=out_shape,
        compiler_params=pltpu.CompilerParams(
            dimension_semantics=("arbitrary",), vmem_limit_bytes=VMEM_LIMIT),
        name=name,
    )(*[w for w, _, _ in casts])


def _mixer_kernel(*refs, seg, nseg, carry, emit_vrows, ncast):
    it = iter(refs)
    take = lambda n: [next(it) for _ in range(n)]
    x_ref = take(1)[0]
    conv0_ref, h0_ref = (None, None) if carry else take(2)
    norm1_ref, cvec_ref = take(2)
    win_ref, wgate_ref, wsp_ref, bsb_ref, wout_ref = take(5)
    cast_src = take(ncast)
    y_ref, convo_ref, ho_ref = take(3)
    vrows_ref = take(1)[0] if emit_vrows else None
    cast_dst = take(ncast)
    xpad, a3, b3, h3, hcar = take(5)
    rows = seg * nseg
    gw = GROUP_WIDTH
    groups_per_seg = seg // SUBLANES

    def vec(k, cols=slice(None)):
        return cvec_ref[k:k + 1, cols]

    def pair_cols(p):
        return slice(p * PAIR, (p + 1) * PAIR)

    state_rows = slice(CARRY_ROWS - CONV_STATE, CARRY_ROWS)
    if carry:
        t = pl.program_id(1)

        @pl.when(t == 0)
        def _():
            xpad[0, state_rows, :] = jnp.zeros((CONV_STATE, gw), _F32)
            hcar[...] = jnp.zeros_like(hcar)

        @pl.when(t > 0)
        def _():
            xpad[0, state_rows, :] = xpad[0, seg + CARRY_ROWS - CONV_STATE:seg + CARRY_ROWS, :]
    else:
        for s in range(nseg):
            xpad[s, state_rows, :] = conv0_ref[s]

    x = x_ref[...]
    xnb = _rms(x, norm1_ref[...]).astype(_BF16)

    def proj(group, p):
        return _dot(xnb, win_ref[group * NPAIR + p])

    def conv(p, xa_p):
        cols = pair_cols(p)
        for s in range(nseg):
            xpad[s, CARRY_ROWS:CARRY_ROWS + seg, cols] = xa_p[s * seg:(s + 1) * seg]
            convo_ref[s, :, cols] = xa_p[(s + 1) * seg - CONV_STATE:(s + 1) * seg]

        def shifted(k):
            parts = [xpad[s, CARRY_ROWS - k:CARRY_ROWS - k + seg, cols] for s in range(nseg)]
            return parts[0] if nseg == 1 else jnp.concatenate(parts, axis=0)

        return (shifted(3) * vec(_CW0, cols) + shifted(2) * vec(_CW0 + 1, cols)
                + shifted(1) * vec(_CW0 + 2, cols) + xa_p * vec(_CW0 + 3, cols) + vec(_CB, cols))

    def gates(p, xc_p):
        g = [_dot(xc_p[:, hh * HEAD_DIM:(hh + 1) * HEAD_DIM].astype(_BF16), wgate_ref[CHUNK_HEADS * p + hh])
             for hh in range(CHUNK_HEADS)]
        return (jnp.concatenate([gh[:, :HEAD_DIM] for gh in g], axis=1),
                jnp.concatenate([gh[:, HEAD_DIM:] for gh in g], axis=1))

    neg_lam = -vec(_LAM)
    softplus = jnp.maximum(neg_lam, 0.0) + jnp.log1p(jnp.exp(-jnp.abs(neg_lam)))
    neg_c_softplus = -LRU_C * softplus

    def lru_coeffs(p, xc_p, g_r, g_i):
        cols = pair_cols(p)
        r = _sigmoid(g_r + vec(_BR, cols))
        ig = _sigmoid(g_i + vec(_BI, cols))
        log_a = r * neg_c_softplus[:, cols]
        a = jnp.exp(log_a)
        m = -jnp.tanh(log_a) * (a * a + 1.0)
        bterm = (m * lax.rsqrt(jnp.maximum(m, _MIN_NORMAL))) * (ig * xc_p)
        for hh in range(CHUNK_HEADS):
            c = CHUNK_HEADS * p + hh
            sub = slice(hh * HEAD_DIM, (hh + 1) * HEAD_DIM)
            a3[:, c * SUBLANES:(c + 1) * SUBLANES, :] = a[:, sub].reshape(rows // SUBLANES, SUBLANES, HEAD_DIM)
            b3[:, c * SUBLANES:(c + 1) * SUBLANES, :] = bterm[:, sub].reshape(rows // SUBLANES, SUBLANES, HEAD_DIM)

    def scan_part(q, hs):
        per = groups_per_seg // SCAN_PARTS
        hs = list(hs)
        for j in range(q * per, (q + 1) * per):
            for rr in range(SUBLANES):
                for s in range(nseg):
                    jj = s * groups_per_seg + j
                    at = a3[jj, pl.ds(rr, HEADS, stride=SUBLANES), :]
                    bt = b3[jj, pl.ds(rr, HEADS, stride=SUBLANES), :]
                    hs[s] = at * hs[s] + bt
                    h3[jj, pl.ds(rr, HEADS, stride=SUBLANES), :] = hs[s]
        return hs

    def y_lru_pair(p):
        return jnp.concatenate(
            [h3[:, c * SUBLANES:(c + 1) * SUBLANES, :].reshape(rows, HEAD_DIM)
             for c in range(CHUNK_HEADS * p, CHUNK_HEADS * (p + 1))],
            axis=1)

    def layernorm_v(vg):
        mu = _rowsum(vg) * (1.0 / gw)
        vc = [g - mu for g in vg]
        rstd = lax.rsqrt(_rowsum([c * c for c in vc]) * (1.0 / gw) + EPS)
        return [vc[p] * rstd * vec(_LNG, pair_cols(p)) + vec(_LNB, pair_cols(p)) for p in range(NPAIR)]

    chunk = min(seg, MLP_CHUNK)
    nchunk = rows // chunk

    def token_mlp(h, vb, gu):
        p, hh = divmod(h, CHUNK_HEADS)
        sub = slice(hh * HEAD_DIM, (hh + 1) * HEAD_DIM)
        vh = jnp.concatenate([vb[p][k * chunk:(k + 1) * chunk, sub] for k in range(nchunk)], axis=1)
        m = _dot(wsp_ref[h, 0:chunk, 0:chunk], vh)
        bias = bsb_ref[h][:chunk]
        mixed = jnp.concatenate(
            [m[:, k * HEAD_DIM:(k + 1) * HEAD_DIM] + bias for k in range(nchunk)], axis=0)
        return gu[p][:, sub] * mixed

    def convert_weights():
        for src, dst in zip(cast_src, cast_dst):
            _convert_slab(src, dst)

    xa0 = proj(0, 0)
    xa1 = proj(0, 1)
    xc0 = conv(0, xa0)
    xa2 = proj(0, 2)
    xc1 = conv(1, xa1)
    gt0 = gates(0, xc0)
    xa3 = proj(0, 3)
    xc2 = conv(2, xa2)
    gt1 = gates(1, xc1)
    v = [None] * NPAIR
    v[0] = proj(3, 0)
    xc3 = conv(3, xa3)
    gt2 = gates(2, xc2)
    v[1] = proj(3, 1)
    lru_coeffs(0, xc0, *gt0)
    gt3 = gates(3, xc3)
    v[2] = proj(3, 2)
    lru_coeffs(1, xc1, *gt1)
    v[3] = proj(3, 3)
    lru_coeffs(2, xc2, *gt2)
    u = [None] * NPAIR
    u[0] = proj(2, 0)
    lru_coeffs(3, xc3, *gt3)

    hs = [hcar[...] if carry else h0_ref[s] for s in range(nseg)]
    u[1] = proj(2, 1)
    hs = scan_part(0, hs)
    vg = [_gelu(v[0]), _gelu(v[1])]
    u[2] = proj(2, 2)
    convert_weights()
    hs = scan_part(1, hs)
    vg += [_gelu(v[2]), _gelu(v[3])]
    u[3] = proj(2, 3)
    hs = scan_part(2, hs)
    v_n = layernorm_v(vg)
    if emit_vrows:
        for p in range(NPAIR):
            vrows_ref[:, pair_cols(p)] = v_n[p]
    vb = [n.astype(_BF16) for n in v_n]
    ga = [None] * NPAIR
    ga[0] = proj(1, 0)
    hs = scan_part(3, hs)
    if carry:
        hcar[...] = hs[0]
    for s in range(nseg):
        ho_ref[s] = hs[s]
    gu = [_gelu(u[p]) for p in range(NPAIR)]
    ga[1] = proj(1, 1)
    out_b = [token_mlp(h, vb, gu) for h in range(HEADS // 2)]
    ga[2] = proj(1, 2)
    out_b += [token_mlp(h, vb, gu) for h in range(HEADS // 2, HEADS)]
    gga = [_gelu(ga[0]), _gelu(ga[1])]
    ga[3] = proj(1, 3)
    rs_b = lax.rsqrt(_rowsum([o * o for o in out_b]) * (1.0 / gw) + EPS)
    nb = jnp.concatenate(
        [out_b[h] * rs_b * vec(_GNB, slice(h * HEAD_DIM, (h + 1) * HEAD_DIM)) for h in range(HEADS)],
        axis=1).astype(_BF16)
    gga.append(_gelu(ga[2]))

    acc = [_dot(nb, wout_ref[n, gw:2 * gw, :]) for n in range(OUT_CHUNKS)]
    gga.append(_gelu(ga[3]))
    out_a = [y_lru_pair(p) * gga[p] for p in range(NPAIR)]
    rs_a = lax.rsqrt(_rowsum([o * o for o in out_a]) * (1.0 / gw) + EPS)
    na = jnp.concatenate([out_a[p] * rs_a * vec(_GNA, pair_cols(p)) for p in range(NPAIR)],
                         axis=1).astype(_BF16)
    for n in range(OUT_CHUNKS):
        cols = slice(n * OUT_COLS, (n + 1) * OUT_COLS)
        y_ref[:, cols] = x[:, cols] + acc[n] + _dot(na, wout_ref[n, 0:gw, :])


def _layer_spec(shape, layer):
    zeros = (0,) * (len(shape) - 1)
    return pl.BlockSpec((None,) + tuple(shape[1:]), lambda *_: (layer,) + zeros,
                        pipeline_mode=pl.Buffered(1))


def _whole_spec(shape):
    zeros = (0,) * len(shape)
    return pl.BlockSpec(tuple(shape), lambda *_: zeros, pipeline_mode=pl.Buffered(1))


def _mixer(x2d, state, norm1, cvec, win, wgate, wsp, bsb, wout, casts=(), *, layer, nseq, seg, nseg,
           carry, emit_vrows, name):
    rows = seg * nseg
    total = x2d.shape[0]
    gw = GROUP_WIDTH
    if carry:
        steps = total // nseq // rows
        grid = (nseq, steps)
        row_map = lambda b, t: (b * steps + t, 0)
        seq_map = lambda b, t: (b, 0, 0)
        state_specs = []
        cast_in, cast_out, cast_shape = _convert_specs(casts, nseq * steps, lambda b, t: b * steps + t)
    else:
        assert not casts
        grid = (total // rows,)
        row_map = lambda i: (i, 0)
        seq_map = lambda i: (i, 0, 0)
        state_map = lambda i: (layer, i, 0, 0)
        state_specs = [pl.BlockSpec((None, nseg, CONV_STATE, gw), state_map),
                       pl.BlockSpec((None, nseg, HEADS, HEAD_DIM), state_map)]
        cast_in, cast_out, cast_shape = [], [], []
    out_shape = [jax.ShapeDtypeStruct((total, D_MODEL), _F32),
                 jax.ShapeDtypeStruct((nseq, CONV_STATE, gw), _F32),
                 jax.ShapeDtypeStruct((nseq, HEADS, HEAD_DIM), _F32)]
    out_specs = [pl.BlockSpec((rows, D_MODEL), row_map),
                 pl.BlockSpec((nseg, CONV_STATE, gw), seq_map),
                 pl.BlockSpec((nseg, HEADS, HEAD_DIM), seq_map)]
    if emit_vrows:
        out_shape.append(jax.ShapeDtypeStruct((total, gw), _F32))
        out_specs.append(pl.BlockSpec((rows, gw), row_map))
    scan_shape = (rows // SUBLANES, HEADS * SUBLANES, HEAD_DIM)
    return pl.pallas_call(
        functools.partial(_mixer_kernel, seg=seg, nseg=nseg, carry=carry, emit_vrows=emit_vrows,
                          ncast=len(casts)),
        grid=grid,
        in_specs=[pl.BlockSpec((rows, D_MODEL), row_map), *state_specs,
                  _layer_spec(norm1.shape, layer), _layer_spec(cvec.shape, layer),
                  _whole_spec(win.shape), _layer_spec(wgate.shape, layer),
                  _layer_spec(wsp.shape, layer), _layer_spec(bsb.shape, layer),
                  _whole_spec(wout.shape), *cast_in],
        out_specs=out_specs + cast_out,
        out_shape=out_shape + cast_shape,
        scratch_shapes=[pltpu.VMEM((nseg, seg + CARRY_ROWS, gw), _F32),
                        pltpu.VMEM(scan_shape, _F32), pltpu.VMEM(scan_shape, _F32),
                        pltpu.VMEM(scan_shape, _F32), pltpu.VMEM((HEADS, HEAD_DIM), _F32)],
        compiler_params=pltpu.CompilerParams(
            dimension_semantics=("arbitrary",) * len(grid), vmem_limit_bytes=VMEM_LIMIT),
        name=name,
    )(x2d, *state, norm1, cvec, win, wgate, wsp, bsb, wout, *[w for w, _, _ in casts])


def _ffn_kernel(h_ref, norm2_ref, wg_ref, wu_ref, *rest, final):
    wd_refs, (normf_ref, o_ref, hn_ref) = rest[:OUT_CHUNKS], rest[OUT_CHUNKS:]
    j = pl.program_id(1)

    @pl.when(j == 0)
    def _():
        h = h_ref[...]
        hn_ref[...] = _rms(h, norm2_ref[...]).astype(_BF16)
        o_ref[...] = h

    hn = hn_ref[...]
    g = _dot(hn, wg_ref[...])
    u = _dot(hn, wu_ref[...])
    act = ((g * _sigmoid(g)) * u).astype(_BF16)
    for n in range(OUT_CHUNKS):
        cols = slice(n * OUT_COLS, (n + 1) * OUT_COLS)
        o_ref[:, cols] += _dot(act, wd_refs[n][...])

    if final:
        @pl.when(j == pl.num_programs(1) - 1)
        def _():
            o_ref[...] = _rms(o_ref[...], normf_ref[...])


def _ffn(h2d, norm2, wg, wu, wd, normf, *, layer, final, name):
    total = h2d.shape[0]
    d_ff = wg.shape[-1]
    rows = min(FFN_ROWS, total)
    grid = (total // rows, d_ff // FFN_COLS)
    return pl.pallas_call(
        functools.partial(_ffn_kernel, final=final),
        grid=grid,
        in_specs=[pl.BlockSpec((rows, D_MODEL), lambda i, j: (i, 0)),
                  _layer_spec(norm2.shape, layer),
                  pl.BlockSpec((D_MODEL, FFN_COLS), lambda i, j: (0, j)),
                  pl.BlockSpec((D_MODEL, FFN_COLS), lambda i, j: (0, j)),
                  *[pl.BlockSpec((FFN_COLS, OUT_COLS), lambda i, j, n=n: (j, n)) for n in range(OUT_CHUNKS)],
                  _layer_spec(normf.shape, 0)],
        out_specs=pl.BlockSpec((rows, D_MODEL), lambda i, j: (i, 0)),
        out_shape=jax.ShapeDtypeStruct((total, D_MODEL), _F32),
        scratch_shapes=[pltpu.VMEM((rows, D_MODEL), _BF16)],
        compiler_params=pltpu.CompilerParams(
            dimension_semantics=("arbitrary", "arbitrary"), vmem_limit_bytes=VMEM_LIMIT),
        name=name,
    )(h2d, norm2, wg, wu, *([wd] * OUT_CHUNKS), normf)


def kernel(x_prompt, x_sample, state_conv, state_lru, norm1, w_in, conv_w, conv_b, w_rgate, b_rgate,
           w_igate, b_igate, lru_param, v_ln_g, v_ln_b, w_spatial, b_spatial, gn_a, gn_b, w_out,
           norm2, w_gate, w_up, w_down, norm_f):
    depth = w_in.shape[0]
    batch, seq, _ = x_prompt.shape
    dec_batch, dec_seq, _ = x_sample.shape
    gw = GROUP_WIDTH

    xp = x_prompt.reshape(batch * seq, D_MODEL)
    xs = x_sample.reshape(dec_batch * dec_seq, D_MODEL)
    sample_state = (state_conv, state_lru.reshape(depth, dec_batch, HEADS, HEAD_DIM))

    row = lambda a: a[:, None, :]
    cvec = jnp.concatenate(
        [conv_w, row(conv_b), row(b_rgate), row(b_igate), row(lru_param), row(v_ln_g), row(v_ln_b),
         row(gn_a), row(gn_b), jnp.zeros((depth, _CVEC_ROWS - 12, gw), _F32)], axis=1)
    wgate = jnp.concatenate([w_rgate, w_igate], axis=-1).astype(_BF16)
    bsb = jnp.broadcast_to(b_spatial[..., None], (depth, HEADS, MLP_CHUNK, HEAD_DIM))
    normf = norm_f.reshape(1, 1, D_MODEL)
    steps = seq // MIXER_ROWS

    def proj_casts(l):
        return [(w_in, l, PAIR), (w_out, l, OUT_COLS)]

    win, wout = _convert(proj_casts(0), CONVERT_STEPS, name="convert_proj_0")
    wsp = _tril(w_spatial)

    conv_p, lru_p, conv_s, lru_s, vrows_s = [], [], [], [], []
    for l in range(depth):
        final = l == depth - 1
        mixer_w = (norm1[:, None, :], cvec, win, wgate, wsp, bsb, wout)
        casts = [(w_gate, l, None), (w_up, l, None), (w_down, l, None)]
        if not final:
            casts += proj_casts(l + 1)
        hp, cp, lp, wg, wu, wd, *nxt = _mixer(xp, (), *mixer_w, casts, layer=l, nseq=batch,
                                              seg=MIXER_ROWS, nseg=1, carry=True, emit_vrows=False,
                                              name=f"mixer_prompt_{l}")
        ffn_w = (norm2[:, None, :], wg, wu, wd, normf)
        hs, cs, ls, vs = _mixer(xs, sample_state, *mixer_w, layer=l, nseq=dec_batch, seg=dec_seq,
                                nseg=MIXER_ROWS // dec_seq, carry=False, emit_vrows=True,
                                name=f"mixer_sample_{l}")
        xp = _ffn(hp, *ffn_w, layer=l, final=final, name=f"ffn_prompt_{l}")
        xs = _ffn(hs, *ffn_w, layer=l, final=final, name=f"ffn_sample_{l}")
        if not final:
            win, wout = nxt

        conv_p.append(cp)
        lru_p.append(lp.reshape(batch, gw))
        conv_s.append(cs)
        lru_s.append(ls.reshape(dec_batch, gw))
        vrows_s.append(vs.reshape(dec_batch, dec_seq, gw))

    return (xp.reshape(batch, seq, D_MODEL), xs.reshape(dec_batch, dec_seq, D_MODEL),
            jnp.stack(conv_p), jnp.stack(lru_p), jnp.stack(conv_s), jnp.stack(lru_s),
            jnp.stack(vrows_s))
```

```python
import functools
import math

import jax
import jax.numpy as jnp
from jax import lax
from jax.experimental import pallas as pl
from jax.experimental.pallas import tpu as pltpu

D_MODEL = 2048
GROUP_WIDTH = D_MODEL // 2
HEADS = 8
HEAD_DIM = GROUP_WIDTH // HEADS
CONV_WIDTH = 4
MLP_CHUNK = 128
LRU_C = 8.0
EPS = 1e-6

SUBLANES = 8
BF16_ROWS = 2 * SUBLANES
CARRY_ROWS = SUBLANES
CONV_STATE = CONV_WIDTH - 1
CHUNK_HEADS = 2
PAIR = CHUNK_HEADS * HEAD_DIM
NPAIR = GROUP_WIDTH // PAIR
SCAN_PARTS = 4
OUT_COLS = 512
OUT_CHUNKS = D_MODEL // OUT_COLS

MIXER_ROWS = 256
FFN_ROWS = 1024
FFN_COLS = 512
VMEM_LIMIT = 60 * 1024 * 1024
CONVERT_STEPS = 8

_CW0, _CB, _BR, _BI, _LAM, _LNG, _LNB, _GNA, _GNB = 0, 4, 5, 6, 7, 8, 9, 10, 11

_BF16 = jnp.bfloat16
_F32 = jnp.float32
_MIN_NORMAL = float(jnp.finfo(jnp.float32).tiny)


def _dot(a, b):
    return jnp.dot(a, b, preferred_element_type=_F32)


def _rms(x, g):
    return x * lax.rsqrt(jnp.mean(x * x, axis=-1, keepdims=True) + EPS) * g


def _gelu(x):
    c = math.sqrt(2.0 / math.pi)
    hx = 0.5 * x
    return hx + hx * jnp.tanh(x * (c + (c * 0.044715) * (x * x)))


def _sigmoid(x):
    return 1.0 / (1.0 + jnp.exp(-x))


def _rowsum(parts):
    total = jnp.sum(parts[0], axis=-1, keepdims=True)
    for p in parts[1:]:
        total = total + jnp.sum(p, axis=-1, keepdims=True)
    return total


def _convert_slab(srcs, dst):
    if len(dst.shape) == 2:
        dst[...] = srcs[0][...].astype(_BF16)
        return
    width = dst.shape[-1]
    for c in range(dst.shape[0]):
        cols = slice(c * width, (c + 1) * width)
        if len(dst.shape) == 3:
            dst[c] = srcs[0][:, cols].astype(_BF16)
        else:
            for k, src in enumerate(srcs):
                dst[c, k] = src[:, cols].astype(_BF16)


def _convert_specs(casts, steps, step_of):
    in_specs, out_specs, out_shape = [], [], []
    for srcs, layer, width in casts:
        _, nrow, ncol = srcs[0].shape
        hold = next(h for h in (1, 2, 4, 8)
                    if nrow * h % steps == 0 and (nrow * h // steps) % BF16_ROWS == 0)
        slab = nrow * hold // steps
        index = lambda *g, hold=hold: step_of(*g) // hold
        in_specs += [pl.BlockSpec((None, slab, ncol),
                                  lambda *g, layer=layer, index=index: (layer, index(*g), 0))] * len(srcs)
        if width is None:
            shape, block = (nrow, ncol), (slab, ncol)
            omap = lambda *g, index=index: (index(*g), 0)
        elif len(srcs) == 1:
            shape, block = (ncol // width, nrow, width), (ncol // width, slab, width)
            omap = lambda *g, index=index: (0, index(*g), 0)
        else:
            shape = (ncol // width, len(srcs), nrow, width)
            block = (ncol // width, len(srcs), slab, width)
            omap = lambda *g, index=index: (0, 0, index(*g), 0)
        out_shape.append(jax.ShapeDtypeStruct(shape, _BF16))
        out_specs.append(pl.BlockSpec(block, omap))
    return in_specs, out_specs, out_shape


def _convert_all(plan, src_refs, dst_refs):
    at = 0
    for nsrc, dst in zip(plan, dst_refs):
        _convert_slab(src_refs[at:at + nsrc], dst)
        at += nsrc


def _tril_kernel(w_ref, o_ref):
    tri = (lax.broadcasted_iota(jnp.int32, w_ref.shape, 1)
           >= lax.broadcasted_iota(jnp.int32, w_ref.shape, 2))
    o_ref[...] = jnp.where(tri, w_ref[...], 0.0).astype(_BF16)


def _tril(w_spatial):
    depth = w_spatial.shape[0]
    block = (None,) + tuple(w_spatial.shape[1:])
    return pl.pallas_call(
        _tril_kernel, grid=(depth,),
        in_specs=[pl.BlockSpec(block, lambda l: (l, 0, 0, 0))],
        out_specs=pl.BlockSpec(block, lambda l: (l, 0, 0, 0)),
        out_shape=jax.ShapeDtypeStruct(w_spatial.shape, _BF16),
        name="tril_spatial",
    )(w_spatial)


def _convert_kernel(*refs, plan):
    nsrc = sum(plan)
    _convert_all(plan, refs[:nsrc], refs[nsrc:])


def _convert(casts, steps, name):
    in_specs, out_specs, out_shape = _convert_specs(casts, steps, lambda i: i)
    return pl.pallas_call(
        functools.partial(_convert_kernel, plan=tuple(len(srcs) for srcs, _, _ in casts)),
        grid=(steps,), in_specs=in_specs, out_specs=out_specs, out_shape=out_shape,
        compiler_params=pltpu.CompilerParams(
            dimension_semantics=("arbitrary",), vmem_limit_bytes=VMEM_LIMIT),
        name=name,
    )(*[w for srcs, _, _ in casts for w in srcs])


def _mixer_kernel(*refs, seg, nseg, carry, emit_vrows, cast_plan):
    it = iter(refs)
    take = lambda n: [next(it) for _ in range(n)]
    x_ref = take(1)[0]
    conv0_ref, h0_ref = (None, None) if carry else take(2)
    norm1_ref, cvec_ref = take(2)
    win_ref, wgate_ref, wsp_ref, bsb_ref, wout_ref = take(5)
    cast_src = take(sum(cast_plan))
    y_ref, convo_ref, ho_ref = take(3)
    vrows_ref = take(1)[0] if emit_vrows else None
    cast_dst = take(len(cast_plan))
    xpad, a3, b3, h3, hcar = take(5)
    rows = seg * nseg
    gw = GROUP_WIDTH
    groups_per_seg = seg // SUBLANES

    def vec(k, cols=slice(None)):
        return cvec_ref[k:k + 1, cols]

    def pair_cols(p):
        return slice(p * PAIR, (p + 1) * PAIR)

    state_rows = slice(CARRY_ROWS - CONV_STATE, CARRY_ROWS)
    if carry:
        t = pl.program_id(1)

        @pl.when(t == 0)
        def _():
            xpad[0, state_rows, :] = jnp.zeros((CONV_STATE, gw), _F32)
            hcar[...] = jnp.zeros_like(hcar)

        @pl.when(t > 0)
        def _():
            xpad[0, state_rows, :] = xpad[0, seg + CARRY_ROWS - CONV_STATE:seg + CARRY_ROWS, :]
    else:
        for s in range(nseg):
            xpad[s, state_rows, :] = conv0_ref[s]

    x = x_ref[...]
    xnb = _rms(x, norm1_ref[...]).astype(_BF16)

    def proj(group, p):
        return _dot(xnb, win_ref[group * NPAIR + p])

    def conv(p, xa_p):
        cols = pair_cols(p)
        for s in range(nseg):
            xpad[s, CARRY_ROWS:CARRY_ROWS + seg, cols] = xa_p[s * seg:(s + 1) * seg]
            convo_ref[s, :, cols] = xa_p[(s + 1) * seg - CONV_STATE:(s + 1) * seg]

        def shifted(k):
            parts = [xpad[s, CARRY_ROWS - k:CARRY_ROWS - k + seg, cols] for s in range(nseg)]
            return parts[0] if nseg == 1 else jnp.concatenate(parts, axis=0)

        return (shifted(3) * vec(_CW0, cols) + shifted(2) * vec(_CW0 + 1, cols)
                + shifted(1) * vec(_CW0 + 2, cols) + xa_p * vec(_CW0 + 3, cols) + vec(_CB, cols))

    def gates(p, xc_p):
        g = [_dot(xc_p[:, hh * HEAD_DIM:(hh + 1) * HEAD_DIM].astype(_BF16), wgate_ref[CHUNK_HEADS * p + hh])
             for hh in range(CHUNK_HEADS)]
        return (jnp.concatenate([gh[:, :HEAD_DIM] for gh in g], axis=1),
                jnp.concatenate([gh[:, HEAD_DIM:] for gh in g], axis=1))

    neg_lam = -vec(_LAM)
    softplus = jnp.maximum(neg_lam, 0.0) + jnp.log1p(jnp.exp(-jnp.abs(neg_lam)))
    neg_c_softplus = -LRU_C * softplus

    def lru_coeffs(p, xc_p, g_r, g_i):
        cols = pair_cols(p)
        r = _sigmoid(g_r + vec(_BR, cols))
        ig = _sigmoid(g_i + vec(_BI, cols))
        log_a = r * neg_c_softplus[:, cols]
        a = jnp.exp(log_a)
        m = -jnp.tanh(log_a) * (a * a + 1.0)
        bterm = (m * lax.rsqrt(jnp.maximum(m, _MIN_NORMAL))) * (ig * xc_p)
        for hh in range(CHUNK_HEADS):
            c = CHUNK_HEADS * p + hh
            sub = slice(hh * HEAD_DIM, (hh + 1) * HEAD_DIM)
            a3[:, c * SUBLANES:(c + 1) * SUBLANES, :] = a[:, sub].reshape(rows // SUBLANES, SUBLANES, HEAD_DIM)
            b3[:, c * SUBLANES:(c + 1) * SUBLANES, :] = bterm[:, sub].reshape(rows // SUBLANES, SUBLANES, HEAD_DIM)

    def scan_part(q, hs):
        per = groups_per_seg // SCAN_PARTS
        hs = list(hs)
        for j in range(q * per, (q + 1) * per):
            for rr in range(SUBLANES):
                for s in range(nseg):
                    jj = s * groups_per_seg + j
                    at = a3[jj, pl.ds(rr, HEADS, stride=SUBLANES), :]
                    bt = b3[jj, pl.ds(rr, HEADS, stride=SUBLANES), :]
                    hs[s] = at * hs[s] + bt
                    h3[jj, pl.ds(rr, HEADS, stride=SUBLANES), :] = hs[s]
        return hs

    def y_lru_pair(p):
        return jnp.concatenate(
            [h3[:, c * SUBLANES:(c + 1) * SUBLANES, :].reshape(rows, HEAD_DIM)
             for c in range(CHUNK_HEADS * p, CHUNK_HEADS * (p + 1))],
            axis=1)

    def layernorm_v(vg):
        mu = _rowsum(vg) * (1.0 / gw)
        vc = [g - mu for g in vg]
        rstd = lax.rsqrt(_rowsum([c * c for c in vc]) * (1.0 / gw) + EPS)
        return [vc[p] * rstd * vec(_LNG, pair_cols(p)) + vec(_LNB, pair_cols(p)) for p in range(NPAIR)]

    chunk = min(seg, MLP_CHUNK)
    nchunk = rows // chunk

    def token_mlp(h, vb, gu):
        p, hh = divmod(h, CHUNK_HEADS)
        sub = slice(hh * HEAD_DIM, (hh + 1) * HEAD_DIM)
        vh = jnp.concatenate([vb[p][k * chunk:(k + 1) * chunk, sub] for k in range(nchunk)], axis=1)
        m = _dot(wsp_ref[h, 0:chunk, 0:chunk], vh)
        bias = bsb_ref[h][:chunk]
        mixed = jnp.concatenate(
            [m[:, k * HEAD_DIM:(k + 1) * HEAD_DIM] + bias for k in range(nchunk)], axis=0)
        return gu[p][:, sub] * mixed

    def convert_weights():
        _convert_all(cast_plan, cast_src, cast_dst)

    xa0 = proj(0, 0)
    xa1 = proj(0, 1)
    xc0 = conv(0, xa0)
    xa2 = proj(0, 2)
    xc1 = conv(1, xa1)
    gt0 = gates(0, xc0)
    xa3 = proj(0, 3)
    xc2 = conv(2, xa2)
    gt1 = gates(1, xc1)
    v = [None] * NPAIR
    v[0] = proj(3, 0)
    xc3 = conv(3, xa3)
    gt2 = gates(2, xc2)
    v[1] = proj(3, 1)
    lru_coeffs(0, xc0, *gt0)
    gt3 = gates(3, xc3)
    v[2] = proj(3, 2)
    lru_coeffs(1, xc1, *gt1)
    v[3] = proj(3, 3)
    lru_coeffs(2, xc2, *gt2)
    u = [None] * NPAIR
    u[0] = proj(2, 0)
    lru_coeffs(3, xc3, *gt3)

    hs = [hcar[...] if carry else h0_ref[s] for s in range(nseg)]
    u[1] = proj(2, 1)
    hs = scan_part(0, hs)
    vg = [_gelu(v[0]), _gelu(v[1])]
    u[2] = proj(2, 2)
    convert_weights()
    hs = scan_part(1, hs)
    vg += [_gelu(v[2]), _gelu(v[3])]
    u[3] = proj(2, 3)
    hs = scan_part(2, hs)
    v_n = layernorm_v(vg)
    if emit_vrows:
        for p in range(NPAIR):
            vrows_ref[:, pair_cols(p)] = v_n[p]
    vb = [n.astype(_BF16) for n in v_n]
    ga = [None] * NPAIR
    ga[0] = proj(1, 0)
    hs = scan_part(3, hs)
    if carry:
        hcar[...] = hs[0]
    for s in range(nseg):
        ho_ref[s] = hs[s]
    gu = [_gelu(u[p]) for p in range(NPAIR)]
    ga[1] = proj(1, 1)
    out_b = [token_mlp(h, vb, gu) for h in range(HEADS // 2)]
    ga[2] = proj(1, 2)
    out_b += [token_mlp(h, vb, gu) for h in range(HEADS // 2, HEADS)]
    gga = [_gelu(ga[0]), _gelu(ga[1])]
    ga[3] = proj(1, 3)
    rs_b = lax.rsqrt(_rowsum([o * o for o in out_b]) * (1.0 / gw) + EPS)
    nb = jnp.concatenate(
        [out_b[h] * rs_b * vec(_GNB, slice(h * HEAD_DIM, (h + 1) * HEAD_DIM)) for h in range(HEADS)],
        axis=1).astype(_BF16)
    gga.append(_gelu(ga[2]))

    acc = [_dot(nb, wout_ref[n, gw:2 * gw, :]) for n in range(OUT_CHUNKS)]
    gga.append(_gelu(ga[3]))
    out_a = [y_lru_pair(p) * gga[p] for p in range(NPAIR)]
    rs_a = lax.rsqrt(_rowsum([o * o for o in out_a]) * (1.0 / gw) + EPS)
    na = jnp.concatenate([out_a[p] * rs_a * vec(_GNA, pair_cols(p)) for p in range(NPAIR)],
                         axis=1).astype(_BF16)
    for n in range(OUT_CHUNKS):
        cols = slice(n * OUT_COLS, (n + 1) * OUT_COLS)
        y_ref[:, cols] = x[:, cols] + acc[n] + _dot(na, wout_ref[n, 0:gw, :])


def _layer_spec(shape, layer):
    zeros = (0,) * (len(shape) - 1)
    return pl.BlockSpec((None,) + tuple(shape[1:]), lambda *_: (layer,) + zeros,
                        pipeline_mode=pl.Buffered(1))


def _whole_spec(shape):
    zeros = (0,) * len(shape)
    return pl.BlockSpec(tuple(shape), lambda *_: zeros, pipeline_mode=pl.Buffered(1))


def _mixer(x2d, state, norm1, cvec, win, wgate, wsp, bsb, wout, casts=(), *, layer, nseq, seg, nseg,
           carry, emit_vrows, name):
    rows = seg * nseg
    total = x2d.shape[0]
    gw = GROUP_WIDTH
    if carry:
        steps = total // nseq // rows
        grid = (nseq, steps)
        row_map = lambda b, t: (b * steps + t, 0)
        seq_map = lambda b, t: (b, 0, 0)
        state_specs = []
        cast_in, cast_out, cast_shape = _convert_specs(casts, nseq * steps, lambda b, t: b * steps + t)
    else:
        assert not casts
        grid = (total // rows,)
        row_map = lambda i: (i, 0)
        seq_map = lambda i: (i, 0, 0)
        state_map = lambda i: (layer, i, 0, 0)
        state_specs = [pl.BlockSpec((None, nseg, CONV_STATE, gw), state_map),
                       pl.BlockSpec((None, nseg, HEADS, HEAD_DIM), state_map)]
        cast_in, cast_out, cast_shape = [], [], []
    out_shape = [jax.ShapeDtypeStruct((total, D_MODEL), _F32),
                 jax.ShapeDtypeStruct((nseq, CONV_STATE, gw), _F32),
                 jax.ShapeDtypeStruct((nseq, HEADS, HEAD_DIM), _F32)]
    out_specs = [pl.BlockSpec((rows, D_MODEL), row_map),
                 pl.BlockSpec((nseg, CONV_STATE, gw), seq_map),
                 pl.BlockSpec((nseg, HEADS, HEAD_DIM), seq_map)]
    if emit_vrows:
        out_shape.append(jax.ShapeDtypeStruct((total, gw), _F32))
        out_specs.append(pl.BlockSpec((rows, gw), row_map))
    scan_shape = (rows // SUBLANES, HEADS * SUBLANES, HEAD_DIM)
    return pl.pallas_call(
        functools.partial(_mixer_kernel, seg=seg, nseg=nseg, carry=carry, emit_vrows=emit_vrows,
                          cast_plan=tuple(len(srcs) for srcs, _, _ in casts)),
        grid=grid,
        in_specs=[pl.BlockSpec((rows, D_MODEL), row_map), *state_specs,
                  _layer_spec(norm1.shape, layer), _layer_spec(cvec.shape, layer),
                  _whole_spec(win.shape), _layer_spec(wgate.shape, layer),
                  _layer_spec(wsp.shape, layer), _layer_spec(bsb.shape, layer),
                  _whole_spec(wout.shape), *cast_in],
        out_specs=out_specs + cast_out,
        out_shape=out_shape + cast_shape,
        scratch_shapes=[pltpu.VMEM((nseg, seg + CARRY_ROWS, gw), _F32),
                        pltpu.VMEM(scan_shape, _F32), pltpu.VMEM(scan_shape, _F32),
                        pltpu.VMEM(scan_shape, _F32), pltpu.VMEM((HEADS, HEAD_DIM), _F32)],
        compiler_params=pltpu.CompilerParams(
            dimension_semantics=("arbitrary",) * len(grid), vmem_limit_bytes=VMEM_LIMIT),
        name=name,
    )(x2d, *state, norm1, cvec, win, wgate, wsp, bsb, wout, *[w for srcs, _, _ in casts for w in srcs])


def _ffn_kernel(h_ref, norm2_ref, wgu_ref, wd_ref, normf_ref, o_ref, hn_ref, *, final):
    j = pl.program_id(1)

    @pl.when(j == 0)
    def _():
        h = h_ref[...]
        hn_ref[...] = _rms(h, norm2_ref[...]).astype(_BF16)
        o_ref[...] = h

    hn = hn_ref[...]
    g = _dot(hn, wgu_ref[0])
    u = _dot(hn, wgu_ref[1])
    act = ((g * _sigmoid(g)) * u).astype(_BF16)
    for n in range(OUT_CHUNKS):
        cols = slice(n * OUT_COLS, (n + 1) * OUT_COLS)
        o_ref[:, cols] += _dot(act, wd_ref[n])

    if final:
        @pl.when(j == pl.num_programs(1) - 1)
        def _():
            o_ref[...] = _rms(o_ref[...], normf_ref[...])


def _ffn(h2d, norm2, wgu, wd, normf, *, layer, final, name):
    total = h2d.shape[0]
    rows = min(FFN_ROWS, total)
    grid = (total // rows, wgu.shape[0])
    return pl.pallas_call(
        functools.partial(_ffn_kernel, final=final),
        grid=grid,
        in_specs=[pl.BlockSpec((rows, D_MODEL), lambda i, j: (i, 0)),
                  _layer_spec(norm2.shape, layer),
                  pl.BlockSpec((None, 2, D_MODEL, FFN_COLS), lambda i, j: (j, 0, 0, 0)),
                  pl.BlockSpec((OUT_CHUNKS, FFN_COLS, OUT_COLS), lambda i, j: (0, j, 0)),
                  _layer_spec(normf.shape, 0)],
        out_specs=pl.BlockSpec((rows, D_MODEL), lambda i, j: (i, 0)),
        out_shape=jax.ShapeDtypeStruct((total, D_MODEL), _F32),
        scratch_shapes=[pltpu.VMEM((rows, D_MODEL), _BF16)],
        compiler_params=pltpu.CompilerParams(
            dimension_semantics=("arbitrary", "arbitrary"), vmem_limit_bytes=VMEM_LIMIT),
        name=name,
    )(h2d, norm2, wgu, wd, normf)


def kernel(x_prompt, x_sample, state_conv, state_lru, norm1, w_in, conv_w, conv_b, w_rgate, b_rgate,
           w_igate, b_igate, lru_param, v_ln_g, v_ln_b, w_spatial, b_spatial, gn_a, gn_b, w_out,
           norm2, w_gate, w_up, w_down, norm_f):
    depth = w_in.shape[0]
    batch, seq, _ = x_prompt.shape
    dec_batch, dec_seq, _ = x_sample.shape
    gw = GROUP_WIDTH

    xp = x_prompt.reshape(batch * seq, D_MODEL)
    xs = x_sample.reshape(dec_batch * dec_seq, D_MODEL)
    sample_state = (state_conv, state_lru.reshape(depth, dec_batch, HEADS, HEAD_DIM))

    row = lambda a: a[:, None, :]
    cvec = jnp.concatenate(
        [conv_w, row(conv_b), row(b_rgate), row(b_igate), row(lru_param), row(v_ln_g), row(v_ln_b),
         row(gn_a), row(gn_b)], axis=1)
    wgate = jnp.concatenate([w_rgate, w_igate], axis=-1).astype(_BF16)
    bsb = jnp.broadcast_to(b_spatial[..., None], (depth, HEADS, MLP_CHUNK, HEAD_DIM))
    normf = norm_f.reshape(1, 1, D_MODEL)
    steps = seq // MIXER_ROWS

    def proj_casts(l):
        return [((w_in,), l, PAIR), ((w_out,), l, OUT_COLS)]

    win, wout = _convert(proj_casts(0), CONVERT_STEPS, name="convert_proj_0")
    wsp = _tril(w_spatial)

    conv_p, lru_p, conv_s, lru_s, vrows_s = [], [], [], [], []
    for l in range(depth):
        final = l == depth - 1
        mixer_w = (norm1[:, None, :], cvec, win, wgate, wsp, bsb, wout)
        casts = [((w_gate, w_up), l, FFN_COLS), ((w_down,), l, OUT_COLS)]
        if not final:
            casts += proj_casts(l + 1)
        hp, cp, lp, wgu, wd, *nxt = _mixer(xp, (), *mixer_w, casts, layer=l, nseq=batch,
                                              seg=MIXER_ROWS, nseg=1, carry=True, emit_vrows=False,
                                              name=f"mixer_prompt_{l}")
        ffn_w = (norm2[:, None, :], wgu, wd, normf)
        hs, cs, ls, vs = _mixer(xs, sample_state, *mixer_w, layer=l, nseq=dec_batch, seg=dec_seq,
                                nseg=MIXER_ROWS // dec_seq, carry=False, emit_vrows=True,
                                name=f"mixer_sample_{l}")
        xp = _ffn(hp, *ffn_w, layer=l, final=final, name=f"ffn_prompt_{l}")
        xs = _ffn(hs, *ffn_w, layer=l, final=final, name=f"ffn_sample_{l}")
        if not final:
            win, wout = nxt

        conv_p.append(cp)
        lru_p.append(lp.reshape(batch, gw))
        conv_s.append(cs)
        lru_s.append(ls.reshape(dec_batch, gw))
        vrows_s.append(vs.reshape(dec_batch, dec_seq, gw))

    return (xp.reshape(batch, seq, D_MODEL), xs.reshape(dec_batch, dec_seq, D_MODEL),
            jnp.stack(conv_p), jnp.stack(lru_p), jnp.stack(conv_s), jnp.stack(lru_s),
            jnp.stack(vrows_s))
```

```python
import functools
import math

import jax
import jax.numpy as jnp
from jax import lax
from jax.experimental import pallas as pl
from jax.experimental.pallas import tpu as pltpu

D_MODEL = 2048
GROUP_WIDTH = D_MODEL // 2
HEADS = 8
HEAD_DIM = GROUP_WIDTH // HEADS
CONV_WIDTH = 4
MLP_CHUNK = 128
LRU_C = 8.0
EPS = 1e-6

SUBLANES = 8
BF16_ROWS = 2 * SUBLANES
CARRY_ROWS = SUBLANES
CONV_STATE = CONV_WIDTH - 1
CHUNK_HEADS = 2
PAIR = CHUNK_HEADS * HEAD_DIM
NPAIR = GROUP_WIDTH // PAIR
SCAN_PARTS = 4
OUT_COLS = 512
OUT_CHUNKS = D_MODEL // OUT_COLS

MIXER_ROWS = 256
FFN_ROWS = 1024
FFN_COLS = 512
VMEM_LIMIT = 60 * 1024 * 1024
CONVERT_STEPS = 8

_CW0, _CB, _BR, _BI, _LAM, _LNG, _LNB, _GNA, _GNB = 0, 4, 5, 6, 7, 8, 9, 10, 11
_CVEC_ROWS = 16

_BF16 = jnp.bfloat16
_F32 = jnp.float32
_MIN_NORMAL = float(jnp.finfo(jnp.float32).tiny)


def _dot(a, b):
    return jnp.dot(a, b, preferred_element_type=_F32)


def _rms(x, g):
    return x * lax.rsqrt(jnp.mean(x * x, axis=-1, keepdims=True) + EPS) * g


def _gelu(x):
    c = math.sqrt(2.0 / math.pi)
    hx = 0.5 * x
    return hx + hx * jnp.tanh(x * (c + (c * 0.044715) * (x * x)))


def _sigmoid(x):
    return 1.0 / (1.0 + jnp.exp(-x))


def _rowsum(parts):
    total = jnp.sum(parts[0], axis=-1, keepdims=True)
    for p in parts[1:]:
        total = total + jnp.sum(p, axis=-1, keepdims=True)
    return total


def _convert_slab(src, dst):
    if len(dst.shape) == 2:
        dst[...] = src[...].astype(_BF16)
    else:
        width = dst.shape[2]
        for c in range(dst.shape[0]):
            dst[c] = src[:, c * width:(c + 1) * width].astype(_BF16)


def _convert_specs(casts, steps, step_of):
    in_specs, out_specs, out_shape = [], [], []
    for w, layer, width in casts:
        _, nrow, ncol = w.shape
        hold = next(h for h in (1, 2, 4, 8)
                    if nrow * h % steps == 0 and (nrow * h // steps) % BF16_ROWS == 0)
        slab = nrow * hold // steps
        index = lambda *g, hold=hold: step_of(*g) // hold
        in_specs.append(pl.BlockSpec((None, slab, ncol),
                                     lambda *g, layer=layer, index=index: (layer, index(*g), 0)))
        if width is None:
            out_shape.append(jax.ShapeDtypeStruct((nrow, ncol), _BF16))
            out_specs.append(pl.BlockSpec((slab, ncol), lambda *g, index=index: (index(*g), 0)))
        else:
            out_shape.append(jax.ShapeDtypeStruct((ncol // width, nrow, width), _BF16))
            out_specs.append(pl.BlockSpec((ncol // width, slab, width),
                                          lambda *g, index=index: (0, index(*g), 0)))
    return in_specs, out_specs, out_shape


def _tril_kernel(w_ref, o_ref):
    tri = (lax.broadcasted_iota(jnp.int32, w_ref.shape, 1)
           >= lax.broadcasted_iota(jnp.int32, w_ref.shape, 2))
    o_ref[...] = jnp.where(tri, w_ref[...], 0.0).astype(_BF16)


def _tril(w_spatial):
    depth = w_spatial.shape[0]
    block = (None,) + tuple(w_spatial.shape[1:])
    return pl.pallas_call(
        _tril_kernel, grid=(depth,),
        in_specs=[pl.BlockSpec(block, lambda l: (l, 0, 0, 0))],
        out_specs=pl.BlockSpec(block, lambda l: (l, 0, 0, 0)),
        out_shape=jax.ShapeDtypeStruct(w_spatial.shape, _BF16),
        name="tril_spatial",
    )(w_spatial)


def _convert_kernel(*refs):
    half = len(refs) // 2
    for src, dst in zip(refs[:half], refs[half:]):
        _convert_slab(src, dst)


def _convert(casts, steps, name):
    in_specs, out_specs, out_shape = _convert_specs(casts, steps, lambda i: i)
    return pl.pallas_call(
        _convert_kernel, grid=(steps,), in_specs=in_specs, out_specs=out_specs, out_shape=out_shape,
        compiler_params=pltpu.CompilerParams(
            dimension_semantics=("arbitrary",), vmem_limit_bytes=VMEM_LIMIT),
        name=name,
    )(*[w for w, _, _ in casts])


def _mixer_kernel(*refs, seg, nseg, carry, emit_vrows, ncast):
    it = iter(refs)
    take = lambda n: [next(it) for _ in range(n)]
    x_ref = take(1)[0]
    conv0_ref, h0_ref = (None, None) if carry else take(2)
    norm1_ref, cvec_ref = take(2)
    win_ref, wgate_ref, wsp_ref, bsb_ref, wout_ref = take(5)
    cast_src = take(ncast)
    y_ref, convo_ref, ho_ref = take(3)
    vrows_ref = take(1)[0] if emit_vrows else None
    cast_dst = take(ncast)
    xpad, a3, b3, h3, hcar = take(5)
    rows = seg * nseg
    gw = GROUP_WIDTH
    groups_per_seg = seg // SUBLANES

    def vec(k, cols=slice(None)):
        return cvec_ref[k:k + 1, cols]

    def pair_cols(p):
        return slice(p * PAIR, (p + 1) * PAIR)

    state_rows = slice(CARRY_ROWS - CONV_STATE, CARRY_ROWS)
    if carry:
        t = pl.program_id(1)

        @pl.when(t == 0)
        def _():
            xpad[0, state_rows, :] = jnp.zeros((CONV_STATE, gw), _F32)
            hcar[...] = jnp.zeros_like(hcar)

        @pl.when(t > 0)
        def _():
            xpad[0, state_rows, :] = xpad[0, seg + CARRY_ROWS - CONV_STATE:seg + CARRY_ROWS, :]
    else:
        for s in range(nseg):
            xpad[s, state_rows, :] = conv0_ref[s]

    x = x_ref[...]
    xnb = _rms(x, norm1_ref[...]).astype(_BF16)

    def proj(group, p):
        return _dot(xnb, win_ref[group * NPAIR + p])

    def conv(p, xa_p):
        cols = pair_cols(p)
        for s in range(nseg):
            xpad[s, CARRY_ROWS:CARRY_ROWS + seg, cols] = xa_p[s * seg:(s + 1) * seg]
            convo_ref[s, :, cols] = xa_p[(s + 1) * seg - CONV_STATE:(s + 1) * seg]

        def shifted(k):
            parts = [xpad[s, CARRY_ROWS - k:CARRY_ROWS - k + seg, cols] for s in range(nseg)]
            return parts[0] if nseg == 1 else jnp.concatenate(parts, axis=0)

        return (shifted(3) * vec(_CW0, cols) + shifted(2) * vec(_CW0 + 1, cols)
                + shifted(1) * vec(_CW0 + 2, cols) + xa_p * vec(_CW0 + 3, cols) + vec(_CB, cols))

    def gates(p, xc_p):
        g = [_dot(xc_p[:, hh * HEAD_DIM:(hh + 1) * HEAD_DIM].astype(_BF16), wgate_ref[CHUNK_HEADS * p + hh])
             for hh in range(CHUNK_HEADS)]
        return (jnp.concatenate([gh[:, :HEAD_DIM] for gh in g], axis=1),
                jnp.concatenate([gh[:, HEAD_DIM:] for gh in g], axis=1))

    neg_lam = -vec(_LAM)
    softplus = jnp.maximum(neg_lam, 0.0) + jnp.log1p(jnp.exp(-jnp.abs(neg_lam)))
    neg_c_softplus = -LRU_C * softplus

    def lru_coeffs(p, xc_p, g_r, g_i):
        cols = pair_cols(p)
        r = _sigmoid(g_r + vec(_BR, cols))
        ig = _sigmoid(g_i + vec(_BI, cols))
        log_a = r * neg_c_softplus[:, cols]
        a = jnp.exp(log_a)
        m = -jnp.tanh(log_a) * (a * a + 1.0)
        bterm = (m * lax.rsqrt(jnp.maximum(m, _MIN_NORMAL))) * (ig * xc_p)
        for hh in range(CHUNK_HEADS):
            c = CHUNK_HEADS * p + hh
            sub = slice(hh * HEAD_DIM, (hh + 1) * HEAD_DIM)
            a3[:, c * SUBLANES:(c + 1) * SUBLANES, :] = a[:, sub].reshape(rows // SUBLANES, SUBLANES, HEAD_DIM)
            b3[:, c * SUBLANES:(c + 1) * SUBLANES, :] = bterm[:, sub].reshape(rows // SUBLANES, SUBLANES, HEAD_DIM)

    def scan_part(q, hs):
        per = groups_per_seg // SCAN_PARTS
        hs = list(hs)
        for j in range(q * per, (q + 1) * per):
            for rr in range(SUBLANES):
                for s in range(nseg):
                    jj = s * groups_per_seg + j
                    at = a3[jj, pl.ds(rr, HEADS, stride=SUBLANES), :]
                    bt = b3[jj, pl.ds(rr, HEADS, stride=SUBLANES), :]
                    hs[s] = at * hs[s] + bt
                    h3[jj, pl.ds(rr, HEADS, stride=SUBLANES), :] = hs[s]
        return hs

    def y_lru_pair(p):
        return jnp.concatenate(
            [h3[:, c * SUBLANES:(c + 1) * SUBLANES, :].reshape(rows, HEAD_DIM)
             for c in range(CHUNK_HEADS * p, CHUNK_HEADS * (p + 1))],
            axis=1)

    def layernorm_v(vg):
        mu = _rowsum(vg) * (1.0 / gw)
        vc = [g - mu for g in vg]
        rstd = lax.rsqrt(_rowsum([c * c for c in vc]) * (1.0 / gw) + EPS)
        return [vc[p] * rstd * vec(_LNG, pair_cols(p)) + vec(_LNB, pair_cols(p)) for p in range(NPAIR)]

    chunk = min(seg, MLP_CHUNK)
    nchunk = rows // chunk

    def token_mlp(h, vb, gu):
        p, hh = divmod(h, CHUNK_HEADS)
        sub = slice(hh * HEAD_DIM, (hh + 1) * HEAD_DIM)
        vh = jnp.concatenate([vb[p][k * chunk:(k + 1) * chunk, sub] for k in range(nchunk)], axis=1)
        m = _dot(wsp_ref[h, 0:chunk, 0:chunk], vh)
        bias = bsb_ref[h][:chunk]
        mixed = jnp.concatenate(
            [m[:, k * HEAD_DIM:(k + 1) * HEAD_DIM] + bias for k in range(nchunk)], axis=0)
        return gu[p][:, sub] * mixed

    def convert_weights():
        for src, dst in zip(cast_src, cast_dst):
            _convert_slab(src, dst)

    xa0 = proj(0, 0)
    xa1 = proj(0, 1)
    xc0 = conv(0, xa0)
    xa2 = proj(0, 2)
    xc1 = conv(1, xa1)
    gt0 = gates(0, xc0)
    xa3 = proj(0, 3)
    xc2 = conv(2, xa2)
    gt1 = gates(1, xc1)
    v = [None] * NPAIR
    v[0] = proj(3, 0)
    xc3 = conv(3, xa3)
    gt2 = gates(2, xc2)
    v[1] = proj(3, 1)
    lru_coeffs(0, xc0, *gt0)
    gt3 = gates(3, xc3)
    v[2] = proj(3, 2)
    lru_coeffs(1, xc1, *gt1)
    v[3] = proj(3, 3)
    lru_coeffs(2, xc2, *gt2)
    u = [None] * NPAIR
    u[0] = proj(2, 0)
    lru_coeffs(3, xc3, *gt3)

    hs = [hcar[...] if carry else h0_ref[s] for s in range(nseg)]
    u[1] = proj(2, 1)
    hs = scan_part(0, hs)
    vg = [_gelu(v[0]), _gelu(v[1])]
    u[2] = proj(2, 2)
    convert_weights()
    hs = scan_part(1, hs)
    vg += [_gelu(v[2]), _gelu(v[3])]
    u[3] = proj(2, 3)
    hs = scan_part(2, hs)
    v_n = layernorm_v(vg)
    if emit_vrows:
        for p in range(NPAIR):
            vrows_ref[:, pair_cols(p)] = v_n[p]
    vb = [n.astype(_BF16) for n in v_n]
    ga = [None] * NPAIR
    ga[0] = proj(1, 0)
    hs = scan_part(3, hs)
    if carry:
        hcar[...] = hs[0]
    for s in range(nseg):
        ho_ref[s] = hs[s]
    gu = [_gelu(u[p]) for p in range(NPAIR)]
    ga[1] = proj(1, 1)
    out_b = [token_mlp(h, vb, gu) for h in range(HEADS // 2)]
    ga[2] = proj(1, 2)
    out_b += [token_mlp(h, vb, gu) for h in range(HEADS // 2, HEADS)]
    gga = [_gelu(ga[0]), _gelu(ga[1])]
    ga[3] = proj(1, 3)
    rs_b = lax.rsqrt(_rowsum([o * o for o in out_b]) * (1.0 / gw) + EPS)
    nb = jnp.concatenate(
        [out_b[h] * rs_b * vec(_GNB, slice(h * HEAD_DIM, (h + 1) * HEAD_DIM)) for h in range(HEADS)],
        axis=1).astype(_BF16)
    gga.append(_gelu(ga[2]))

    acc = [_dot(nb, wout_ref[n, gw:2 * gw, :]) for n in range(OUT_CHUNKS)]
    gga.append(_gelu(ga[3]))
    out_a = [y_lru_pair(p) * gga[p] for p in range(NPAIR)]
    rs_a = lax.rsqrt(_rowsum([o * o for o in out_a]) * (1.0 / gw) + EPS)
    na = jnp.concatenate([out_a[p] * rs_a * vec(_GNA, pair_cols(p)) for p in range(NPAIR)],
                         axis=1).astype(_BF16)
    for n in range(OUT_CHUNKS):
        cols = slice(n * OUT_COLS, (n + 1) * OUT_COLS)
        y_ref[:, cols] = x[:, cols] + acc[n] + _dot(na, wout_ref[n, 0:gw, :])


def _layer_spec(shape, layer):
    zeros = (0,) * (len(shape) - 1)
    return pl.BlockSpec((None,) + tuple(shape[1:]), lambda *_: (layer,) + zeros,
                        pipeline_mode=pl.Buffered(1))


def _whole_spec(shape):
    zeros = (0,) * len(shape)
    return pl.BlockSpec(tuple(shape), lambda *_: zeros, pipeline_mode=pl.Buffered(1))


def _mixer(x2d, state, norm1, cvec, win, wgate, wsp, bsb, wout, casts=(), *, layer, nseq, seg, nseg,
           carry, emit_vrows, name):
    rows = seg * nseg
    total = x2d.shape[0]
    gw = GROUP_WIDTH
    if carry:
        steps = total // nseq // rows
        grid = (nseq, steps)
        row_map = lambda b, t: (b * steps + t, 0)
        seq_map = lambda b, t: (b, 0, 0)
        state_specs = []
        cast_in, cast_out, cast_shape = _convert_specs(casts, nseq * steps, lambda b, t: b * steps + t)
    else:
        assert not casts
        grid = (total // rows,)
        row_map = lambda i: (i, 0)
        seq_map = lambda i: (i, 0, 0)
        state_map = lambda i: (layer, i, 0, 0)
        state_specs = [pl.BlockSpec((None, nseg, CONV_STATE, gw), state_map),
                       pl.BlockSpec((None, nseg, HEADS, HEAD_DIM), state_map)]
        cast_in, cast_out, cast_shape = [], [], []
    out_shape = [jax.ShapeDtypeStruct((total, D_MODEL), _F32),
                 jax.ShapeDtypeStruct((nseq, CONV_STATE, gw), _F32),
                 jax.ShapeDtypeStruct((nseq, HEADS, HEAD_DIM), _F32)]
    out_specs = [pl.BlockSpec((rows, D_MODEL), row_map),
                 pl.BlockSpec((nseg, CONV_STATE, gw), seq_map),
                 pl.BlockSpec((nseg, HEADS, HEAD_DIM), seq_map)]
    if emit_vrows:
        out_shape.append(jax.ShapeDtypeStruct((total, gw), _F32))
        out_specs.append(pl.BlockSpec((rows, gw), row_map))
    scan_shape = (rows // SUBLANES, HEADS * SUBLANES, HEAD_DIM)
    return pl.pallas_call(
        functools.partial(_mixer_kernel, seg=seg, nseg=nseg, carry=carry, emit_vrows=emit_vrows,
                          ncast=len(casts)),
        grid=grid,
        in_specs=[pl.BlockSpec((rows, D_MODEL), row_map), *state_specs,
                  _layer_spec(norm1.shape, layer), _layer_spec(cvec.shape, layer),
                  _whole_spec(win.shape), _layer_spec(wgate.shape, layer),
                  _layer_spec(wsp.shape, layer), _layer_spec(bsb.shape, layer),
                  _whole_spec(wout.shape), *cast_in],
        out_specs=out_specs + cast_out,
        out_shape=out_shape + cast_shape,
        scratch_shapes=[pltpu.VMEM((nseg, seg + CARRY_ROWS, gw), _F32),
                        pltpu.VMEM(scan_shape, _F32), pltpu.VMEM(scan_shape, _F32),
                        pltpu.VMEM(scan_shape, _F32), pltpu.VMEM((HEADS, HEAD_DIM), _F32)],
        compiler_params=pltpu.CompilerParams(
            dimension_semantics=("arbitrary",) * len(grid), vmem_limit_bytes=VMEM_LIMIT),
        name=name,
    )(x2d, *state, norm1, cvec, win, wgate, wsp, bsb, wout, *[w for w, _, _ in casts])


def _ffn_kernel(h_ref, norm2_ref, wg_ref, wu_ref, *rest, final):
    wd_refs, (normf_ref, o_ref, hn_ref) = rest[:OUT_CHUNKS], rest[OUT_CHUNKS:]
    j = pl.program_id(1)

    def add_slab(base_ref):
        hn = hn_ref[...]
        g = _dot(hn, wg_ref[...])
        u = _dot(hn, wu_ref[...])
        act = ((g * _sigmoid(g)) * u).astype(_BF16)
        for n in range(OUT_CHUNKS):
            cols = slice(n * OUT_COLS, (n + 1) * OUT_COLS)
            o_ref[:, cols] = base_ref[:, cols] + _dot(act, wd_refs[n][...])

    @pl.when(j == 0)
    def _():
        hn_ref[...] = _rms(h_ref[...], norm2_ref[...]).astype(_BF16)
        add_slab(h_ref)

    @pl.when(j > 0)
    def _():
        add_slab(o_ref)

    if final:
        @pl.when(j == pl.num_programs(1) - 1)
        def _():
            o_ref[...] = _rms(o_ref[...], normf_ref[...])


def _ffn(h2d, norm2, wg, wu, wd, normf, *, layer, final, name):
    total = h2d.shape[0]
    d_ff = wg.shape[-1]
    rows = min(FFN_ROWS, total)
    grid = (total // rows, d_ff // FFN_COLS)
    return pl.pallas_call(
        functools.partial(_ffn_kernel, final=final),
        grid=grid,
        in_specs=[pl.BlockSpec((rows, D_MODEL), lambda i, j: (i, 0)),
                  _layer_spec(norm2.shape, layer),
                  pl.BlockSpec((D_MODEL, FFN_COLS), lambda i, j: (0, j)),
                  pl.BlockSpec((D_MODEL, FFN_COLS), lambda i, j: (0, j)),
                  *[pl.BlockSpec((FFN_COLS, OUT_COLS), lambda i, j, n=n: (j, n)) for n in range(OUT_CHUNKS)],
                  _layer_spec(normf.shape, 0)],
        out_specs=pl.BlockSpec((rows, D_MODEL), lambda i, j: (i, 0)),
        out_shape=jax.ShapeDtypeStruct((total, D_MODEL), _F32),
        scratch_shapes=[pltpu.VMEM((rows, D_MODEL), _BF16)],
        compiler_params=pltpu.CompilerParams(
            dimension_semantics=("arbitrary", "arbitrary"), vmem_limit_bytes=VMEM_LIMIT),
        name=name,
    )(h2d, norm2, wg, wu, *([wd] * OUT_CHUNKS), normf)


def kernel(x_prompt, x_sample, state_conv, state_lru, norm1, w_in, conv_w, conv_b, w_rgate, b_rgate,
           w_igate, b_igate, lru_param, v_ln_g, v_ln_b, w_spatial, b_spatial, gn_a, gn_b, w_out,
           norm2, w_gate, w_up, w_down, norm_f):
    depth = w_in.shape[0]
    batch, seq, _ = x_prompt.shape
    dec_batch, dec_seq, _ = x_sample.shape
    gw = GROUP_WIDTH

    xp = x_prompt.reshape(batch * seq, D_MODEL)
    xs = x_sample.reshape(dec_batch * dec_seq, D_MODEL)
    sample_state = (state_conv, state_lru.reshape(depth, dec_batch, HEADS, HEAD_DIM))

    row = lambda a: a[:, None, :]
    cvec = jnp.concatenate(
        [conv_w, row(conv_b), row(b_rgate), row(b_igate), row(lru_param), row(v_ln_g), row(v_ln_b),
         row(gn_a), row(gn_b), jnp.zeros((depth, _CVEC_ROWS - 12, gw), _F32)], axis=1)
    wgate = jnp.concatenate([w_rgate, w_igate], axis=-1).astype(_BF16)
    bsb = jnp.broadcast_to(b_spatial[..., None], (depth, HEADS, MLP_CHUNK, HEAD_DIM))
    normf = norm_f.reshape(1, 1, D_MODEL)
    steps = seq // MIXER_ROWS

    def proj_casts(l):
        return [(w_in, l, PAIR), (w_out, l, OUT_COLS)]

    win, wout = _convert(proj_casts(0), CONVERT_STEPS, name="convert_proj_0")
    wsp = _tril(w_spatial)

    conv_p, lru_p, conv_s, lru_s, vrows_s = [], [], [], [], []
    for l in range(depth):
        final = l == depth - 1
        mixer_w = (norm1[:, None, :], cvec, win, wgate, wsp, bsb, wout)
        casts = [(w_gate, l, None), (w_up, l, None), (w_down, l, None)]
        if not final:
            casts += proj_casts(l + 1)
        hp, cp, lp, wg, wu, wd, *nxt = _mixer(xp, (), *mixer_w, casts, layer=l, nseq=batch,
                                              seg=MIXER_ROWS, nseg=1, carry=True, emit_vrows=False,
                                              name=f"mixer_prompt_{l}")
        ffn_w = (norm2[:, None, :], wg, wu, wd, normf)
        hs, cs, ls, vs = _mixer(xs, sample_state, *mixer_w, layer=l, nseq=dec_batch, seg=dec_seq,
                                nseg=MIXER_ROWS // dec_seq, carry=False, emit_vrows=True,
                                name=f"mixer_sample_{l}")
        xp = _ffn(hp, *ffn_w, layer=l, final=final, name=f"ffn_prompt_{l}")
        xs = _ffn(hs, *ffn_w, layer=l, final=final, name=f"ffn_sample_{l}")
        if not final:
            win, wout = nxt

        conv_p.append(cp)
        lru_p.append(lp.reshape(batch, gw))
        conv_s.append(cs)
        lru_s.append(ls.reshape(dec_batch, gw))
        vrows_s.append(vs.reshape(dec_batch, dec_seq, gw))

    return (xp.reshape(batch, seq, D_MODEL), xs.reshape(dec_batch, dec_seq, D_MODEL),
            jnp.stack(conv_p), jnp.stack(lru_p), jnp.stack(conv_s), jnp.stack(lru_s),
            jnp.stack(vrows_s))
```

```python
import functools
import math

import jax
import jax.numpy as jnp
from jax import lax
from jax.experimental import pallas as pl
from jax.experimental.pallas import tpu as pltpu

D_MODEL = 2048
GROUP_WIDTH = D_MODEL // 2
HEADS = 8
HEAD_DIM = GROUP_WIDTH // HEADS
CONV_WIDTH = 4
MLP_CHUNK = 128
LRU_C = 8.0
EPS = 1e-6

SUBLANES = 8
BF16_ROWS = 2 * SUBLANES
CARRY_ROWS = SUBLANES
CONV_STATE = CONV_WIDTH - 1
CHUNK_HEADS = 2
PAIR = CHUNK_HEADS * HEAD_DIM
NPAIR = GROUP_WIDTH // PAIR
SCAN_PARTS = 4
OUT_COLS = 512
OUT_CHUNKS = D_MODEL // OUT_COLS

MIXER_ROWS = 256
FFN_ROWS = 1024
FFN_COLS = 512
VMEM_LIMIT = 60 * 1024 * 1024
CONVERT_STEPS = 8

_CW0, _CB, _BR, _BI, _LAM, _LNG, _LNB, _GNA, _GNB = 0, 4, 5, 6, 7, 8, 9, 10, 11
_CVEC_ROWS = 16

_BF16 = jnp.bfloat16
_F32 = jnp.float32
_MIN_NORMAL = float(jnp.finfo(jnp.float32).tiny)


def _dot(a, b):
    return jnp.dot(a, b, preferred_element_type=_F32)


def _rms(x, g):
    return x * lax.rsqrt(jnp.mean(x * x, axis=-1, keepdims=True) + EPS) * g


def _gelu(x):
    c = math.sqrt(2.0 / math.pi)
    hx = 0.5 * x
    return hx + hx * jnp.tanh(x * (c + (c * 0.044715) * (x * x)))


def _sigmoid(x):
    return 1.0 / (1.0 + jnp.exp(-x))


def _rowsum(parts):
    total = jnp.sum(parts[0], axis=-1, keepdims=True)
    for p in parts[1:]:
        total = total + jnp.sum(p, axis=-1, keepdims=True)
    return total


def _convert_slab(src, dst):
    if len(dst.shape) == 2:
        dst[...] = src[...].astype(_BF16)
    else:
        width = dst.shape[2]
        for c in range(dst.shape[0]):
            dst[c] = src[:, c * width:(c + 1) * width].astype(_BF16)


def _convert_specs(casts, steps, step_of):
    in_specs, out_specs, out_shape = [], [], []
    for w, layer, width in casts:
        _, nrow, ncol = w.shape
        hold = next(h for h in (1, 2, 4, 8)
                    if nrow * h % steps == 0 and (nrow * h // steps) % BF16_ROWS == 0)
        slab = nrow * hold // steps
        index = lambda *g, hold=hold: step_of(*g) // hold
        in_specs.append(pl.BlockSpec((None, slab, ncol),
                                     lambda *g, layer=layer, index=index: (layer, index(*g), 0)))
        if width is None:
            out_shape.append(jax.ShapeDtypeStruct((nrow, ncol), _BF16))
            out_specs.append(pl.BlockSpec((slab, ncol), lambda *g, index=index: (index(*g), 0)))
        else:
            out_shape.append(jax.ShapeDtypeStruct((ncol // width, nrow, width), _BF16))
            out_specs.append(pl.BlockSpec((ncol // width, slab, width),
                                          lambda *g, index=index: (0, index(*g), 0)))
    return in_specs, out_specs, out_shape


def _tril_kernel(w_ref, o_ref):
    tri = (lax.broadcasted_iota(jnp.int32, w_ref.shape, 1)
           >= lax.broadcasted_iota(jnp.int32, w_ref.shape, 2))
    o_ref[...] = jnp.where(tri, w_ref[...], 0.0).astype(_BF16)


def _tril(w_spatial):
    depth = w_spatial.shape[0]
    block = (None,) + tuple(w_spatial.shape[1:])
    return pl.pallas_call(
        _tril_kernel, grid=(depth,),
        in_specs=[pl.BlockSpec(block, lambda l: (l, 0, 0, 0))],
        out_specs=pl.BlockSpec(block, lambda l: (l, 0, 0, 0)),
        out_shape=jax.ShapeDtypeStruct(w_spatial.shape, _BF16),
        name="tril_spatial",
    )(w_spatial)


def _convert_kernel(*refs):
    half = len(refs) // 2
    for src, dst in zip(refs[:half], refs[half:]):
        _convert_slab(src, dst)


def _convert(casts, steps, name):
    in_specs, out_specs, out_shape = _convert_specs(casts, steps, lambda i: i)
    return pl.pallas_call(
        _convert_kernel, grid=(steps,), in_specs=in_specs, out_specs=out_specs, out_shape=out_shape,
        compiler_params=pltpu.CompilerParams(
            dimension_semantics=("arbitrary",), vmem_limit_bytes=VMEM_LIMIT),
        name=name,
    )(*[w for w, _, _ in casts])


def _mixer_kernel(*refs, seg, nseg, carry, emit_vrows, ncast):
    it = iter(refs)
    take = lambda n: [next(it) for _ in range(n)]
    x_ref = take(1)[0]
    conv0_ref, h0_ref = (None, None) if carry else take(2)
    norm1_ref, cvec_ref = take(2)
    win_ref, wgate_ref, wsp_ref, bsb_ref, wout_ref = take(5)
    cast_src = take(ncast)
    y_ref, convo_ref, ho_ref = take(3)
    vrows_ref = take(1)[0] if emit_vrows else None
    cast_dst = take(ncast)
    xpad, a3, b3, h3, hcar = take(5)
    rows = seg * nseg
    gw = GROUP_WIDTH
    groups_per_seg = seg // SUBLANES

    def vec(k, cols=slice(None)):
        return cvec_ref[k:k + 1, cols]

    def pair_cols(p):
        return slice(p * PAIR, (p + 1) * PAIR)

    state_rows = slice(CARRY_ROWS - CONV_STATE, CARRY_ROWS)
    if carry:
        t = pl.program_id(1)

        @pl.when(t == 0)
        def _():
            xpad[0, state_rows, :] = jnp.zeros((CONV_STATE, gw), _F32)
            hcar[...] = jnp.zeros_like(hcar)

        @pl.when(t > 0)
        def _():
            xpad[0, state_rows, :] = xpad[0, seg + CARRY_ROWS - CONV_STATE:seg + CARRY_ROWS, :]
    else:
        for s in range(nseg):
            xpad[s, state_rows, :] = conv0_ref[s]

    x = x_ref[...]
    xnb = _rms(x, norm1_ref[...]).astype(_BF16)

    def proj(group, p):
        return _dot(xnb, win_ref[group * NPAIR + p])

    def conv(p, xa_p):
        cols = pair_cols(p)
        for s in range(nseg):
            xpad[s, CARRY_ROWS:CARRY_ROWS + seg, cols] = xa_p[s * seg:(s + 1) * seg]
            convo_ref[s, :, cols] = xa_p[(s + 1) * seg - CONV_STATE:(s + 1) * seg]

        def shifted(k):
            parts = [xpad[s, CARRY_ROWS - k:CARRY_ROWS - k + seg, cols] for s in range(nseg)]
            return parts[0] if nseg == 1 else jnp.concatenate(parts, axis=0)

        return (shifted(3) * vec(_CW0, cols) + shifted(2) * vec(_CW0 + 1, cols)
                + shifted(1) * vec(_CW0 + 2, cols) + xa_p * vec(_CW0 + 3, cols) + vec(_CB, cols))

    def gates(p, xc_p):
        g = [_dot(xc_p[:, hh * HEAD_DIM:(hh + 1) * HEAD_DIM].astype(_BF16), wgate_ref[CHUNK_HEADS * p + hh])
             for hh in range(CHUNK_HEADS)]
        return (jnp.concatenate([gh[:, :HEAD_DIM] for gh in g], axis=1),
                jnp.concatenate([gh[:, HEAD_DIM:] for gh in g], axis=1))

    neg_lam = -vec(_LAM)
    softplus = jnp.maximum(neg_lam, 0.0) + jnp.log1p(jnp.exp(-jnp.abs(neg_lam)))
    neg_c_softplus = -LRU_C * softplus

    def lru_coeffs(p, xc_p, g_r, g_i):
        cols = pair_cols(p)
        r = _sigmoid(g_r + vec(_BR, cols))
        ig = _sigmoid(g_i + vec(_BI, cols))
        log_a = r * neg_c_softplus[:, cols]
        a = jnp.exp(log_a)
        m = -jnp.tanh(log_a) * (a * a + 1.0)
        bterm = (m * lax.rsqrt(jnp.maximum(m, _MIN_NORMAL))) * (ig * xc_p)
        for hh in range(CHUNK_HEADS):
            c = CHUNK_HEADS * p + hh
            sub = slice(hh * HEAD_DIM, (hh + 1) * HEAD_DIM)
            a3[:, c * SUBLANES:(c + 1) * SUBLANES, :] = a[:, sub].reshape(rows // SUBLANES, SUBLANES, HEAD_DIM)
            b3[:, c * SUBLANES:(c + 1) * SUBLANES, :] = bterm[:, sub].reshape(rows // SUBLANES, SUBLANES, HEAD_DIM)

    def scan_part(q, hs):
        per = groups_per_seg // SCAN_PARTS
        hs = list(hs)
        for j in range(q * per, (q + 1) * per):
            for rr in range(SUBLANES):
                for s in range(nseg):
                    jj = s * groups_per_seg + j
                    at = a3[jj, pl.ds(rr, HEADS, stride=SUBLANES), :]
                    bt = b3[jj, pl.ds(rr, HEADS, stride=SUBLANES), :]
                    hs[s] = at * hs[s] + bt
                    h3[jj, pl.ds(rr, HEADS, stride=SUBLANES), :] = hs[s]
        return hs

    def y_lru_pair(p):
        return jnp.concatenate(
            [h3[:, c * SUBLANES:(c + 1) * SUBLANES, :].reshape(rows, HEAD_DIM)
             for c in range(CHUNK_HEADS * p, CHUNK_HEADS * (p + 1))],
            axis=1)

    def layernorm_v(vg):
        mu = _rowsum(vg) * (1.0 / gw)
        vc = [g - mu for g in vg]
        rstd = lax.rsqrt(_rowsum([c * c for c in vc]) * (1.0 / gw) + EPS)
        return [vc[p] * rstd * vec(_LNG, pair_cols(p)) + vec(_LNB, pair_cols(p)) for p in range(NPAIR)]

    chunk = min(seg, MLP_CHUNK)
    nchunk = rows // chunk

    def token_mlp(h, vb, gu):
        p, hh = divmod(h, CHUNK_HEADS)
        sub = slice(hh * HEAD_DIM, (hh + 1) * HEAD_DIM)
        vh = jnp.concatenate([vb[p][k * chunk:(k + 1) * chunk, sub] for k in range(nchunk)], axis=1)
        m = _dot(wsp_ref[h, 0:chunk, 0:chunk], vh)
        bias = bsb_ref[h][:chunk]
        mixed = jnp.concatenate(
            [m[:, k * HEAD_DIM:(k + 1) * HEAD_DIM] + bias for k in range(nchunk)], axis=0)
        return gu[p][:, sub] * mixed

    def convert_weights():
        for src, dst in zip(cast_src, cast_dst):
            _convert_slab(src, dst)

    xa0 = proj(0, 0)
    xa1 = proj(0, 1)
    xc0 = conv(0, xa0)
    xa2 = proj(0, 2)
    xc1 = conv(1, xa1)
    gt0 = gates(0, xc0)
    xa3 = proj(0, 3)
    xc2 = conv(2, xa2)
    gt1 = gates(1, xc1)
    v = [None] * NPAIR
    v[0] = proj(3, 0)
    xc3 = conv(3, xa3)
    gt2 = gates(2, xc2)
    v[1] = proj(3, 1)
    lru_coeffs(0, xc0, *gt0)
    gt3 = gates(3, xc3)
    v[2] = proj(3, 2)
    lru_coeffs(1, xc1, *gt1)
    v[3] = proj(3, 3)
    lru_coeffs(2, xc2, *gt2)
    u = [None] * NPAIR
    u[0] = proj(2, 0)
    lru_coeffs(3, xc3, *gt3)

    hs = [hcar[...] if carry else h0_ref[s] for s in range(nseg)]
    u[1] = proj(2, 1)
    hs = scan_part(0, hs)
    vg = [_gelu(v[0]), _gelu(v[1])]
    u[2] = proj(2, 2)
    convert_weights()
    hs = scan_part(1, hs)
    vg += [_gelu(v[2]), _gelu(v[3])]
    u[3] = proj(2, 3)
    hs = scan_part(2, hs)
    v_n = layernorm_v(vg)
    if emit_vrows:
        for p in range(NPAIR):
            vrows_ref[:, pair_cols(p)] = v_n[p]
    vb = [n.astype(_BF16) for n in v_n]
    ga = [None] * NPAIR
    ga[0] = proj(1, 0)
    hs = scan_part(3, hs)
    if carry:
        hcar[...] = hs[0]
    for s in range(nseg):
        ho_ref[s] = hs[s]
    gu = [_gelu(u[p]) for p in range(NPAIR)]
    ga[1] = proj(1, 1)
    out_b = [token_mlp(h, vb, gu) for h in range(HEADS // 2)]
    ga[2] = proj(1, 2)
    out_b += [token_mlp(h, vb, gu) for h in range(HEADS // 2, HEADS)]
    gga = [_gelu(ga[0]), _gelu(ga[1])]
    ga[3] = proj(1, 3)
    rs_b = lax.rsqrt(_rowsum([o * o for o in out_b]) * (1.0 / gw) + EPS)
    nb = jnp.concatenate(
        [out_b[h] * rs_b * vec(_GNB, slice(h * HEAD_DIM, (h + 1) * HEAD_DIM)) for h in range(HEADS)],
        axis=1).astype(_BF16)
    gga.append(_gelu(ga[2]))

    acc = [_dot(nb, wout_ref[n, gw:2 * gw, :]) for n in range(OUT_CHUNKS)]
    gga.append(_gelu(ga[3]))
    out_a = [y_lru_pair(p) * gga[p] for p in range(NPAIR)]
    rs_a = lax.rsqrt(_rowsum([o * o for o in out_a]) * (1.0 / gw) + EPS)
    na = jnp.concatenate([out_a[p] * rs_a * vec(_GNA, pair_cols(p)) for p in range(NPAIR)],
                         axis=1).astype(_BF16)
    for n in range(OUT_CHUNKS):
        cols = slice(n * OUT_COLS, (n + 1) * OUT_COLS)
        y_ref[:, cols] = x[:, cols] + acc[n] + _dot(na, wout_ref[n, 0:gw, :])


def _layer_spec(shape, layer):
    zeros = (0,) * (len(shape) - 1)
    return pl.BlockSpec((None,) + tuple(shape[1:]), lambda *_: (layer,) + zeros,
                        pipeline_mode=pl.Buffered(1))


def _whole_spec(shape):
    zeros = (0,) * len(shape)
    return pl.BlockSpec(tuple(shape), lambda *_: zeros, pipeline_mode=pl.Buffered(1))


def _mixer(x2d, state, norm1, cvec, win, wgate, wsp, bsb, wout, casts=(), *, layer, nseq, seg, nseg,
           carry, emit_vrows, name):
    rows = seg * nseg
    total = x2d.shape[0]
    gw = GROUP_WIDTH
    if carry:
        steps = total // nseq // rows
        grid = (nseq, steps)
        row_map = lambda b, t: (b * steps + t, 0)
        seq_map = lambda b, t: (b, 0, 0)
        state_specs = []
        cast_in, cast_out, cast_shape = _convert_specs(casts, nseq * steps, lambda b, t: b * steps + t)
    else:
        assert not casts
        grid = (total // rows,)
        row_map = lambda i: (i, 0)
        seq_map = lambda i: (i, 0, 0)
        state_map = lambda i: (layer, i, 0, 0)
        state_specs = [pl.BlockSpec((None, nseg, CONV_STATE, gw), state_map),
                       pl.BlockSpec((None, nseg, HEADS, HEAD_DIM), state_map)]
        cast_in, cast_out, cast_shape = [], [], []
    out_shape = [jax.ShapeDtypeStruct((total, D_MODEL), _F32),
                 jax.ShapeDtypeStruct((nseq, CONV_STATE, gw), _F32),
                 jax.ShapeDtypeStruct((nseq, HEADS, HEAD_DIM), _F32)]
    out_specs = [pl.BlockSpec((rows, D_MODEL), row_map),
                 pl.BlockSpec((nseg, CONV_STATE, gw), seq_map),
                 pl.BlockSpec((nseg, HEADS, HEAD_DIM), seq_map)]
    if emit_vrows:
        out_shape.append(jax.ShapeDtypeStruct((total, gw), _F32))
        out_specs.append(pl.BlockSpec((rows, gw), row_map))
    scan_shape = (rows // SUBLANES, HEADS * SUBLANES, HEAD_DIM)
    return pl.pallas_call(
        functools.partial(_mixer_kernel, seg=seg, nseg=nseg, carry=carry, emit_vrows=emit_vrows,
                          ncast=len(casts)),
        grid=grid,
        in_specs=[pl.BlockSpec((rows, D_MODEL), row_map), *state_specs,
                  _layer_spec(norm1.shape, layer), _layer_spec(cvec.shape, layer),
                  _whole_spec(win.shape), _layer_spec(wgate.shape, layer),
                  _layer_spec(wsp.shape, layer), _layer_spec(bsb.shape, layer),
                  _whole_spec(wout.shape), *cast_in],
        out_specs=out_specs + cast_out,
        out_shape=out_shape + cast_shape,
        scratch_shapes=[pltpu.VMEM((nseg, seg + CARRY_ROWS, gw), _F32),
                        pltpu.VMEM(scan_shape, _F32), pltpu.VMEM(scan_shape, _F32),
                        pltpu.VMEM(scan_shape, _F32), pltpu.VMEM((HEADS, HEAD_DIM), _F32)],
        compiler_params=pltpu.CompilerParams(
            dimension_semantics=("arbitrary",) * len(grid), vmem_limit_bytes=VMEM_LIMIT),
        name=name,
    )(x2d, *state, norm1, cvec, win, wgate, wsp, bsb, wout, *[w for w, _, _ in casts])


def _ffn_kernel(h_ref, norm2_ref, wg_ref, wu_ref, *rest, final):
    wd_refs, (normf_ref, o_ref, hn_ref) = rest[:OUT_CHUNKS], rest[OUT_CHUNKS:]
    j = pl.program_id(1)

    last = pl.num_programs(1) - 1

    def add_slab(base_ref, normalise=False):
        hn = hn_ref[...]
        g = _dot(hn, wg_ref[...])
        u = _dot(hn, wu_ref[...])
        act = ((g * _sigmoid(g)) * u).astype(_BF16)
        chunks = [slice(n * OUT_COLS, (n + 1) * OUT_COLS) for n in range(OUT_CHUNKS)]
        outs = [base_ref[:, cols] + _dot(act, wd_refs[n][...]) for n, cols in enumerate(chunks)]
        if normalise:
            rs = lax.rsqrt(_rowsum([o * o for o in outs]) * (1.0 / D_MODEL) + EPS)
            outs = [o * rs * normf_ref[:, cols] for o, cols in zip(outs, chunks)]
        for o, cols in zip(outs, chunks):
            o_ref[:, cols] = o

    @pl.when(j == 0)
    def _():
        hn_ref[...] = _rms(h_ref[...], norm2_ref[...]).astype(_BF16)
        add_slab(h_ref)

    @pl.when((j > 0) & (j < last) if final else (j > 0))
    def _():
        add_slab(o_ref)

    if final:
        @pl.when(j == last)
        def _():
            add_slab(o_ref, normalise=True)


def _ffn(h2d, norm2, wg, wu, wd, normf, *, layer, final, name):
    total = h2d.shape[0]
    d_ff = wg.shape[-1]
    rows = min(FFN_ROWS, total)
    grid = (total // rows, d_ff // FFN_COLS)
    return pl.pallas_call(
        functools.partial(_ffn_kernel, final=final),
        grid=grid,
        in_specs=[pl.BlockSpec((rows, D_MODEL), lambda i, j: (i, 0)),
                  _layer_spec(norm2.shape, layer),
                  pl.BlockSpec((D_MODEL, FFN_COLS), lambda i, j: (0, j)),
                  pl.BlockSpec((D_MODEL, FFN_COLS), lambda i, j: (0, j)),
                  *[pl.BlockSpec((FFN_COLS, OUT_COLS), lambda i, j, n=n: (j, n)) for n in range(OUT_CHUNKS)],
                  _layer_spec(normf.shape, 0)],
        out_specs=pl.BlockSpec((rows, D_MODEL), lambda i, j: (i, 0)),
        out_shape=jax.ShapeDtypeStruct((total, D_MODEL), _F32),
        scratch_shapes=[pltpu.VMEM((rows, D_MODEL), _BF16)],
        compiler_params=pltpu.CompilerParams(
            dimension_semantics=("arbitrary", "arbitrary"), vmem_limit_bytes=VMEM_LIMIT),
        name=name,
    )(h2d, norm2, wg, wu, *([wd] * OUT_CHUNKS), normf)


def kernel(x_prompt, x_sample, state_conv, state_lru, norm1, w_in, conv_w, conv_b, w_rgate, b_rgate,
           w_igate, b_igate, lru_param, v_ln_g, v_ln_b, w_spatial, b_spatial, gn_a, gn_b, w_out,
           norm2, w_gate, w_up, w_down, norm_f):
    depth = w_in.shape[0]
    batch, seq, _ = x_prompt.shape
    dec_batch, dec_seq, _ = x_sample.shape
    gw = GROUP_WIDTH

    xp = x_prompt.reshape(batch * seq, D_MODEL)
    xs = x_sample.reshape(dec_batch * dec_seq, D_MODEL)
    sample_state = (state_conv, state_lru.reshape(depth, dec_batch, HEADS, HEAD_DIM))

    row = lambda a: a[:, None, :]
    cvec = jnp.concatenate(
        [conv_w, row(conv_b), row(b_rgate), row(b_igate), row(lru_param), row(v_ln_g), row(v_ln_b),
         row(gn_a), row(gn_b), jnp.zeros((depth, _CVEC_ROWS - 12, gw), _F32)], axis=1)
    wgate = jnp.concatenate([w_rgate, w_igate], axis=-1).astype(_BF16)
    bsb = jnp.broadcast_to(b_spatial[..., None], (depth, HEADS, MLP_CHUNK, HEAD_DIM))
    normf = norm_f.reshape(1, 1, D_MODEL)
    steps = seq // MIXER_ROWS

    def proj_casts(l):
        return [(w_in, l, PAIR), (w_out, l, OUT_COLS)]

    win, wout = _convert(proj_casts(0), CONVERT_STEPS, name="convert_proj_0")
    wsp = _tril(w_spatial)

    conv_p, lru_p, conv_s, lru_s, vrows_s = [], [], [], [], []
    for l in range(depth):
        final = l == depth - 1
        mixer_w = (norm1[:, None, :], cvec, win, wgate, wsp, bsb, wout)
        casts = [(w_gate, l, None), (w_up, l, None), (w_down, l, None)]
        if not final:
            casts += proj_casts(l + 1)
        hp, cp, lp, wg, wu, wd, *nxt = _mixer(xp, (), *mixer_w, casts, layer=l, nseq=batch,
                                              seg=MIXER_ROWS, nseg=1, carry=True, emit_vrows=False,
                                              name=f"mixer_prompt_{l}")
        ffn_w = (norm2[:, None, :], wg, wu, wd, normf)
        hs, cs, ls, vs = _mixer(xs, sample_state, *mixer_w, layer=l, nseq=dec_batch, seg=dec_seq,
                                nseg=MIXER_ROWS // dec_seq, carry=False, emit_vrows=True,
                                name=f"mixer_sample_{l}")
        xp = _ffn(hp, *ffn_w, layer=l, final=final, name=f"ffn_prompt_{l}")
        xs = _ffn(hs, *ffn_w, layer=l, final=final, name=f"ffn_sample_{l}")
        if not final:
            win, wout = nxt

        conv_p.append(cp)
        lru_p.append(lp.reshape(batch, gw))
        conv_s.append(cs)
        lru_s.append(ls.reshape(dec_batch, gw))
        vrows_s.append(vs.reshape(dec_batch, dec_seq, gw))

    return (xp.reshape(batch, seq, D_MODEL), xs.reshape(dec_batch, dec_seq, D_MODEL),
            jnp.stack(conv_p), jnp.stack(lru_p), jnp.stack(conv_s), jnp.stack(lru_s),
            jnp.stack(vrows_s))
```

```python
import functools
import math

import jax
import jax.numpy as jnp
from jax import lax
from jax.experimental import pallas as pl
from jax.experimental.pallas import tpu as pltpu

D_MODEL = 2048
GROUP_WIDTH = D_MODEL // 2
HEADS = 8
HEAD_DIM = GROUP_WIDTH // HEADS
CONV_WIDTH = 4
MLP_CHUNK = 128
LRU_C = 8.0
EPS = 1e-6

SUBLANES = 8
BF16_ROWS = 2 * SUBLANES
CARRY_ROWS = SUBLANES
CONV_STATE = CONV_WIDTH - 1
CHUNK_HEADS = 2
PAIR = CHUNK_HEADS * HEAD_DIM
NPAIR = GROUP_WIDTH // PAIR
SCAN_PARTS = 4
OUT_COLS = 512
OUT_CHUNKS = D_MODEL // OUT_COLS

MIXER_ROWS = 256
FFN_ROWS = 1024
FFN_COLS = 512
VMEM_LIMIT = 60 * 1024 * 1024
CONVERT_STEPS = 8

_CW0, _CB, _BR, _BI, _LAM, _LNG, _LNB, _GNA, _GNB = 0, 4, 5, 6, 7, 8, 9, 10, 11

_BF16 = jnp.bfloat16
_F32 = jnp.float32
_MIN_NORMAL = float(jnp.finfo(jnp.float32).tiny)


def _dot(a, b):
    return jnp.dot(a, b, preferred_element_type=_F32)


def _rms(x, g):
    return x * lax.rsqrt(jnp.mean(x * x, axis=-1, keepdims=True) + EPS) * g


def _gelu(x):
    c = math.sqrt(2.0 / math.pi)
    hx = 0.5 * x
    return hx + hx * jnp.tanh(x * (c + (c * 0.044715) * (x * x)))


def _sigmoid(x):
    return 1.0 / (1.0 + jnp.exp(-x))


def _rowsum(parts):
    total = jnp.sum(parts[0], axis=-1, keepdims=True)
    for p in parts[1:]:
        total = total + jnp.sum(p, axis=-1, keepdims=True)
    return total


def _convert_slab(src, dst):
    if len(dst.shape) == 2:
        dst[...] = src[...].astype(_BF16)
    else:
        width = dst.shape[2]
        for c in range(dst.shape[0]):
            dst[c] = src[:, c * width:(c + 1) * width].astype(_BF16)


def _convert_specs(casts, steps, step_of):
    in_specs, out_specs, out_shape = [], [], []
    for w, layer, width in casts:
        _, nrow, ncol = w.shape
        hold = next(h for h in (1, 2, 4, 8)
                    if nrow * h % steps == 0 and (nrow * h // steps) % BF16_ROWS == 0)
        slab = nrow * hold // steps
        index = lambda *g, hold=hold: step_of(*g) // hold
        in_specs.append(pl.BlockSpec((None, slab, ncol),
                                     lambda *g, layer=layer, index=index: (layer, index(*g), 0)))
        if width is None:
            out_shape.append(jax.ShapeDtypeStruct((nrow, ncol), _BF16))
            out_specs.append(pl.BlockSpec((slab, ncol), lambda *g, index=index: (index(*g), 0)))
        else:
            out_shape.append(jax.ShapeDtypeStruct((ncol // width, nrow, width), _BF16))
            out_specs.append(pl.BlockSpec((ncol // width, slab, width),
                                          lambda *g, index=index: (0, index(*g), 0)))
    return in_specs, out_specs, out_shape


def _prepare_kernel(convw_ref, convb_ref, br_ref, bi_ref, lam_ref, lng_ref, lnb_ref, gna_ref, gnb_ref,
                    wr_ref, wi_ref, wsp_ref, n1_ref, n2_ref,
                    cvec_ref, wgate_ref, tril_ref, n1_out, n2_out):
    rows = (convb_ref, br_ref, bi_ref, lam_ref, lng_ref, lnb_ref, gna_ref, gnb_ref)
    for l in range(convw_ref.shape[0]):
        cvec_ref[l, _CW0:_CW0 + CONV_WIDTH, :] = convw_ref[l]
        for k, r in enumerate(rows):
            cvec_ref[l, _CB + k:_CB + k + 1, :] = r[l:l + 1, :]
        n1_out[l] = n1_ref[l:l + 1, :]
        n2_out[l] = n2_ref[l:l + 1, :]
    wgate_ref[:, :, :, 0:HEAD_DIM] = wr_ref[...].astype(_BF16)
    wgate_ref[:, :, :, HEAD_DIM:2 * HEAD_DIM] = wi_ref[...].astype(_BF16)
    tri = (lax.broadcasted_iota(jnp.int32, wsp_ref.shape, 2)
           >= lax.broadcasted_iota(jnp.int32, wsp_ref.shape, 3))
    tril_ref[...] = jnp.where(tri, wsp_ref[...], 0.0).astype(_BF16)


def _prepare(conv_w, conv_b, b_rgate, b_igate, lru_param, v_ln_g, v_ln_b, gn_a, gn_b,
             w_rgate, w_igate, w_spatial, norm1, norm2):
    depth, heads, d, _ = w_rgate.shape
    gw = conv_b.shape[1]
    return pl.pallas_call(
        _prepare_kernel,
        out_shape=[jax.ShapeDtypeStruct((depth, _GNB + 1, gw), _F32),
                   jax.ShapeDtypeStruct((depth, heads, d, 2 * d), _BF16),
                   jax.ShapeDtypeStruct(w_spatial.shape, _BF16),
                   jax.ShapeDtypeStruct((depth, 1, norm1.shape[1]), _F32),
                   jax.ShapeDtypeStruct((depth, 1, norm2.shape[1]), _F32)],
        name="prepare_params",
    )(conv_w, conv_b, b_rgate, b_igate, lru_param, v_ln_g, v_ln_b, gn_a, gn_b,
      w_rgate, w_igate, w_spatial, norm1, norm2)


def _convert_kernel(*refs):
    half = len(refs) // 2
    for src, dst in zip(refs[:half], refs[half:]):
        _convert_slab(src, dst)


def _convert(casts, steps, name):
    in_specs, out_specs, out_shape = _convert_specs(casts, steps, lambda i: i)
    return pl.pallas_call(
        _convert_kernel, grid=(steps,), in_specs=in_specs, out_specs=out_specs, out_shape=out_shape,
        compiler_params=pltpu.CompilerParams(
            dimension_semantics=("arbitrary",), vmem_limit_bytes=VMEM_LIMIT),
        name=name,
    )(*[w for w, _, _ in casts])


def _mixer_kernel(*refs, seg, nseg, carry, emit_vrows, ncast):
    it = iter(refs)
    take = lambda n: [next(it) for _ in range(n)]
    x_ref = take(1)[0]
    conv0_ref, h0_ref = (None, None) if carry else take(2)
    norm1_ref, cvec_ref = take(2)
    win_ref, wgate_ref, wsp_ref, bsb_ref, wout_ref = take(5)
    cast_src = take(ncast)
    y_ref, convo_ref, ho_ref = take(3)
    vrows_ref = take(1)[0] if emit_vrows else None
    cast_dst = take(ncast)
    xpad, a3, b3, h3, hcar = take(5)
    rows = seg * nseg
    gw = GROUP_WIDTH
    groups_per_seg = seg // SUBLANES

    def vec(k, cols=slice(None)):
        return cvec_ref[k:k + 1, cols]

    def pair_cols(p):
        return slice(p * PAIR, (p + 1) * PAIR)

    state_rows = slice(CARRY_ROWS - CONV_STATE, CARRY_ROWS)
    if carry:
        t = pl.program_id(1)

        @pl.when(t == 0)
        def _():
            xpad[0, state_rows, :] = jnp.zeros((CONV_STATE, gw), _F32)
            hcar[...] = jnp.zeros_like(hcar)

        @pl.when(t > 0)
        def _():
            xpad[0, state_rows, :] = xpad[0, seg + CARRY_ROWS - CONV_STATE:seg + CARRY_ROWS, :]
    else:
        for s in range(nseg):
            xpad[s, state_rows, :] = conv0_ref[s]

    x = x_ref[...]
    xnb = _rms(x, norm1_ref[...]).astype(_BF16)

    def proj(group, p):
        return _dot(xnb, win_ref[group * NPAIR + p])

    def conv(p, xa_p):
        cols = pair_cols(p)
        for s in range(nseg):
            xpad[s, CARRY_ROWS:CARRY_ROWS + seg, cols] = xa_p[s * seg:(s + 1) * seg]
            convo_ref[s, :, cols] = xa_p[(s + 1) * seg - CONV_STATE:(s + 1) * seg]

        def shifted(k):
            parts = [xpad[s, CARRY_ROWS - k:CARRY_ROWS - k + seg, cols] for s in range(nseg)]
            return parts[0] if nseg == 1 else jnp.concatenate(parts, axis=0)

        return (shifted(3) * vec(_CW0, cols) + shifted(2) * vec(_CW0 + 1, cols)
                + shifted(1) * vec(_CW0 + 2, cols) + xa_p * vec(_CW0 + 3, cols) + vec(_CB, cols))

    def gates(p, xc_p):
        g = [_dot(xc_p[:, hh * HEAD_DIM:(hh + 1) * HEAD_DIM].astype(_BF16), wgate_ref[CHUNK_HEADS * p + hh])
             for hh in range(CHUNK_HEADS)]
        return (jnp.concatenate([gh[:, :HEAD_DIM] for gh in g], axis=1),
                jnp.concatenate([gh[:, HEAD_DIM:] for gh in g], axis=1))

    neg_lam = -vec(_LAM)
    softplus = jnp.maximum(neg_lam, 0.0) + jnp.log1p(jnp.exp(-jnp.abs(neg_lam)))
    neg_c_softplus = -LRU_C * softplus

    def lru_coeffs(p, xc_p, g_r, g_i):
        cols = pair_cols(p)
        r = _sigmoid(g_r + vec(_BR, cols))
        ig = _sigmoid(g_i + vec(_BI, cols))
        log_a = r * neg_c_softplus[:, cols]
        a = jnp.exp(log_a)
        m = -jnp.tanh(log_a) * (a * a + 1.0)
        bterm = (m * lax.rsqrt(jnp.maximum(m, _MIN_NORMAL))) * (ig * xc_p)
        for hh in range(CHUNK_HEADS):
            c = CHUNK_HEADS * p + hh
            sub = slice(hh * HEAD_DIM, (hh + 1) * HEAD_DIM)
            a3[:, c * SUBLANES:(c + 1) * SUBLANES, :] = a[:, sub].reshape(rows // SUBLANES, SUBLANES, HEAD_DIM)
            b3[:, c * SUBLANES:(c + 1) * SUBLANES, :] = bterm[:, sub].reshape(rows // SUBLANES, SUBLANES, HEAD_DIM)

    def scan_part(q, hs):
        per = groups_per_seg // SCAN_PARTS
        hs = list(hs)
        for j in range(q * per, (q + 1) * per):
            for rr in range(SUBLANES):
                for s in range(nseg):
                    jj = s * groups_per_seg + j
                    at = a3[jj, pl.ds(rr, HEADS, stride=SUBLANES), :]
                    bt = b3[jj, pl.ds(rr, HEADS, stride=SUBLANES), :]
                    hs[s] = at * hs[s] + bt
                    h3[jj, pl.ds(rr, HEADS, stride=SUBLANES), :] = hs[s]
        return hs

    def y_lru_pair(p):
        return jnp.concatenate(
            [h3[:, c * SUBLANES:(c + 1) * SUBLANES, :].reshape(rows, HEAD_DIM)
             for c in range(CHUNK_HEADS * p, CHUNK_HEADS * (p + 1))],
            axis=1)

    def layernorm_v(vg):
        mu = _rowsum(vg) * (1.0 / gw)
        vc = [g - mu for g in vg]
        rstd = lax.rsqrt(_rowsum([c * c for c in vc]) * (1.0 / gw) + EPS)
        return [vc[p] * rstd * vec(_LNG, pair_cols(p)) + vec(_LNB, pair_cols(p)) for p in range(NPAIR)]

    chunk = min(seg, MLP_CHUNK)
    nchunk = rows // chunk

    def token_mlp(h, vb, gu):
        p, hh = divmod(h, CHUNK_HEADS)
        sub = slice(hh * HEAD_DIM, (hh + 1) * HEAD_DIM)
        vh = jnp.concatenate([vb[p][k * chunk:(k + 1) * chunk, sub] for k in range(nchunk)], axis=1)
        m = _dot(wsp_ref[h, 0:chunk, 0:chunk], vh)
        bias = bsb_ref[h][:chunk]
        mixed = jnp.concatenate(
            [m[:, k * HEAD_DIM:(k + 1) * HEAD_DIM] + bias for k in range(nchunk)], axis=0)
        return gu[p][:, sub] * mixed

    def convert_weights():
        for src, dst in zip(cast_src, cast_dst):
            _convert_slab(src, dst)

    xa0 = proj(0, 0)
    xa1 = proj(0, 1)
    xc0 = conv(0, xa0)
    xa2 = proj(0, 2)
    xc1 = conv(1, xa1)
    gt0 = gates(0, xc0)
    xa3 = proj(0, 3)
    xc2 = conv(2, xa2)
    gt1 = gates(1, xc1)
    v = [None] * NPAIR
    v[0] = proj(3, 0)
    xc3 = conv(3, xa3)
    gt2 = gates(2, xc2)
    v[1] = proj(3, 1)
    lru_coeffs(0, xc0, *gt0)
    gt3 = gates(3, xc3)
    v[2] = proj(3, 2)
    lru_coeffs(1, xc1, *gt1)
    v[3] = proj(3, 3)
    lru_coeffs(2, xc2, *gt2)
    u = [None] * NPAIR
    u[0] = proj(2, 0)
    lru_coeffs(3, xc3, *gt3)

    hs = [hcar[...] if carry else h0_ref[s] for s in range(nseg)]
    u[1] = proj(2, 1)
    hs = scan_part(0, hs)
    vg = [_gelu(v[0]), _gelu(v[1])]
    u[2] = proj(2, 2)
    convert_weights()
    hs = scan_part(1, hs)
    vg += [_gelu(v[2]), _gelu(v[3])]
    u[3] = proj(2, 3)
    hs = scan_part(2, hs)
    v_n = layernorm_v(vg)
    if emit_vrows:
        for p in range(NPAIR):
            vrows_ref[:, pair_cols(p)] = v_n[p]
    vb = [n.astype(_BF16) for n in v_n]
    ga = [None] * NPAIR
    ga[0] = proj(1, 0)
    hs = scan_part(3, hs)
    if carry:
        hcar[...] = hs[0]
    for s in range(nseg):
        ho_ref[s] = hs[s]
    gu = [_gelu(u[p]) for p in range(NPAIR)]
    ga[1] = proj(1, 1)
    out_b = [token_mlp(h, vb, gu) for h in range(HEADS // 2)]
    ga[2] = proj(1, 2)
    out_b += [token_mlp(h, vb, gu) for h in range(HEADS // 2, HEADS)]
    gga = [_gelu(ga[0]), _gelu(ga[1])]
    ga[3] = proj(1, 3)
    rs_b = lax.rsqrt(_rowsum([o * o for o in out_b]) * (1.0 / gw) + EPS)
    nb = jnp.concatenate(
        [out_b[h] * rs_b * vec(_GNB, slice(h * HEAD_DIM, (h + 1) * HEAD_DIM)) for h in range(HEADS)],
        axis=1).astype(_BF16)
    gga.append(_gelu(ga[2]))

    acc = [_dot(nb, wout_ref[n, gw:2 * gw, :]) for n in range(OUT_CHUNKS)]
    gga.append(_gelu(ga[3]))
    out_a = [y_lru_pair(p) * gga[p] for p in range(NPAIR)]
    rs_a = lax.rsqrt(_rowsum([o * o for o in out_a]) * (1.0 / gw) + EPS)
    na = jnp.concatenate([out_a[p] * rs_a * vec(_GNA, pair_cols(p)) for p in range(NPAIR)],
                         axis=1).astype(_BF16)
    for n in range(OUT_CHUNKS):
        cols = slice(n * OUT_COLS, (n + 1) * OUT_COLS)
        y_ref[:, cols] = x[:, cols] + acc[n] + _dot(na, wout_ref[n, 0:gw, :])


def _layer_spec(shape, layer):
    zeros = (0,) * (len(shape) - 1)
    return pl.BlockSpec((None,) + tuple(shape[1:]), lambda *_: (layer,) + zeros,
                        pipeline_mode=pl.Buffered(1))


def _whole_spec(shape):
    zeros = (0,) * len(shape)
    return pl.BlockSpec(tuple(shape), lambda *_: zeros, pipeline_mode=pl.Buffered(1))


def _mixer(x2d, state, norm1, cvec, win, wgate, wsp, bsb, wout, casts=(), *, layer, nseq, seg, nseg,
           carry, emit_vrows, name):
    rows = seg * nseg
    total = x2d.shape[0]
    gw = GROUP_WIDTH
    if carry:
        steps = total // nseq // rows
        grid = (nseq, steps)
        row_map = lambda b, t: (b * steps + t, 0)
        seq_map = lambda b, t: (b, 0, 0)
        state_specs = []
        cast_in, cast_out, cast_shape = _convert_specs(casts, nseq * steps, lambda b, t: b * steps + t)
    else:
        assert not casts
        grid = (total // rows,)
        row_map = lambda i: (i, 0)
        seq_map = lambda i: (i, 0, 0)
        state_map = lambda i: (layer, i, 0, 0)
        state_specs = [pl.BlockSpec((None, nseg, CONV_STATE, gw), state_map),
                       pl.BlockSpec((None, nseg, HEADS, HEAD_DIM), state_map)]
        cast_in, cast_out, cast_shape = [], [], []
    out_shape = [jax.ShapeDtypeStruct((total, D_MODEL), _F32),
                 jax.ShapeDtypeStruct((nseq, CONV_STATE, gw), _F32),
                 jax.ShapeDtypeStruct((nseq, HEADS, HEAD_DIM), _F32)]
    out_specs = [pl.BlockSpec((rows, D_MODEL), row_map),
                 pl.BlockSpec((nseg, CONV_STATE, gw), seq_map),
                 pl.BlockSpec((nseg, HEADS, HEAD_DIM), seq_map)]
    if emit_vrows:
        out_shape.append(jax.ShapeDtypeStruct((total, gw), _F32))
        out_specs.append(pl.BlockSpec((rows, gw), row_map))
    scan_shape = (rows // SUBLANES, HEADS * SUBLANES, HEAD_DIM)
    return pl.pallas_call(
        functools.partial(_mixer_kernel, seg=seg, nseg=nseg, carry=carry, emit_vrows=emit_vrows,
                          ncast=len(casts)),
        grid=grid,
        in_specs=[pl.BlockSpec((rows, D_MODEL), row_map), *state_specs,
                  _layer_spec(norm1.shape, layer), _layer_spec(cvec.shape, layer),
                  _whole_spec(win.shape), _layer_spec(wgate.shape, layer),
                  _layer_spec(wsp.shape, layer), _layer_spec(bsb.shape, layer),
                  _whole_spec(wout.shape), *cast_in],
        out_specs=out_specs + cast_out,
        out_shape=out_shape + cast_shape,
        scratch_shapes=[pltpu.VMEM((nseg, seg + CARRY_ROWS, gw), _F32),
                        pltpu.VMEM(scan_shape, _F32), pltpu.VMEM(scan_shape, _F32),
                        pltpu.VMEM(scan_shape, _F32), pltpu.VMEM((HEADS, HEAD_DIM), _F32)],
        compiler_params=pltpu.CompilerParams(
            dimension_semantics=("arbitrary",) * len(grid), vmem_limit_bytes=VMEM_LIMIT),
        name=name,
    )(x2d, *state, norm1, cvec, win, wgate, wsp, bsb, wout, *[w for w, _, _ in casts])


def _ffn_kernel(h_ref, norm2_ref, wg_ref, wu_ref, *rest, final):
    wd_refs, (normf_ref, o_ref, hn_ref) = rest[:OUT_CHUNKS], rest[OUT_CHUNKS:]
    j = pl.program_id(1)

    last = pl.num_programs(1) - 1

    def add_slab(base_ref, normalise=False):
        hn = hn_ref[...]
        g = _dot(hn, wg_ref[...])
        u = _dot(hn, wu_ref[...])
        act = ((g * _sigmoid(g)) * u).astype(_BF16)
        chunks = [slice(n * OUT_COLS, (n + 1) * OUT_COLS) for n in range(OUT_CHUNKS)]
        outs = [base_ref[:, cols] + _dot(act, wd_refs[n][...]) for n, cols in enumerate(chunks)]
        if normalise:
            rs = lax.rsqrt(_rowsum([o * o for o in outs]) * (1.0 / D_MODEL) + EPS)
            outs = [o * rs * normf_ref[:, cols] for o, cols in zip(outs, chunks)]
        for o, cols in zip(outs, chunks):
            o_ref[:, cols] = o

    @pl.when(j == 0)
    def _():
        hn_ref[...] = _rms(h_ref[...], norm2_ref[...]).astype(_BF16)
        add_slab(h_ref)

    @pl.when((j > 0) & (j < last) if final else (j > 0))
    def _():
        add_slab(o_ref)

    if final:
        @pl.when(j == last)
        def _():
            add_slab(o_ref, normalise=True)


def _ffn(h2d, norm2, wg, wu, wd, normf, *, layer, final, name):
    total = h2d.shape[0]
    d_ff = wg.shape[-1]
    rows = min(FFN_ROWS, total)
    grid = (total // rows, d_ff // FFN_COLS)
    return pl.pallas_call(
        functools.partial(_ffn_kernel, final=final),
        grid=grid,
        in_specs=[pl.BlockSpec((rows, D_MODEL), lambda i, j: (i, 0)),
                  _layer_spec(norm2.shape, layer),
                  pl.BlockSpec((D_MODEL, FFN_COLS), lambda i, j: (0, j)),
                  pl.BlockSpec((D_MODEL, FFN_COLS), lambda i, j: (0, j)),
                  *[pl.BlockSpec((FFN_COLS, OUT_COLS), lambda i, j, n=n: (j, n)) for n in range(OUT_CHUNKS)],
                  _layer_spec(normf.shape, 0)],
        out_specs=pl.BlockSpec((rows, D_MODEL), lambda i, j: (i, 0)),
        out_shape=jax.ShapeDtypeStruct((total, D_MODEL), _F32),
        scratch_shapes=[pltpu.VMEM((rows, D_MODEL), _BF16)],
        compiler_params=pltpu.CompilerParams(
            dimension_semantics=("arbitrary", "arbitrary"), vmem_limit_bytes=VMEM_LIMIT),
        name=name,
    )(h2d, norm2, wg, wu, *([wd] * OUT_CHUNKS), normf)


def kernel(x_prompt, x_sample, state_conv, state_lru, norm1, w_in, conv_w, conv_b, w_rgate, b_rgate,
           w_igate, b_igate, lru_param, v_ln_g, v_ln_b, w_spatial, b_spatial, gn_a, gn_b, w_out,
           norm2, w_gate, w_up, w_down, norm_f):
    depth = w_in.shape[0]
    batch, seq, _ = x_prompt.shape
    dec_batch, dec_seq, _ = x_sample.shape
    gw = GROUP_WIDTH

    xp = x_prompt.reshape(batch * seq, D_MODEL)
    xs = x_sample.reshape(dec_batch * dec_seq, D_MODEL)
    sample_state = (state_conv, state_lru.reshape(depth, dec_batch, HEADS, HEAD_DIM))

    cvec, wgate, wsp, norm1p, norm2p = _prepare(
        conv_w, conv_b, b_rgate, b_igate, lru_param, v_ln_g, v_ln_b, gn_a, gn_b,
        w_rgate, w_igate, w_spatial, norm1, norm2)
    bsb = jnp.broadcast_to(b_spatial[..., None], (depth, HEADS, MLP_CHUNK, HEAD_DIM))
    normf = norm_f.reshape(1, 1, D_MODEL)
    steps = seq // MIXER_ROWS

    def proj_casts(l):
        return [(w_in, l, PAIR), (w_out, l, OUT_COLS)]

    win, wout = _convert(proj_casts(0), CONVERT_STEPS, name="convert_proj_0")

    conv_p, lru_p, conv_s, lru_s, vrows_s = [], [], [], [], []
    for l in range(depth):
        final = l == depth - 1
        mixer_w = (norm1p, cvec, win, wgate, wsp, bsb, wout)
        casts = [(w_gate, l, None), (w_up, l, None), (w_down, l, None)]
        if not final:
            casts += proj_casts(l + 1)
        hp, cp, lp, wg, wu, wd, *nxt = _mixer(xp, (), *mixer_w, casts, layer=l, nseq=batch,
                                              seg=MIXER_ROWS, nseg=1, carry=True, emit_vrows=False,
                                              name=f"mixer_prompt_{l}")
        ffn_w = (norm2p, wg, wu, wd, normf)
        hs, cs, ls, vs = _mixer(xs, sample_state, *mixer_w, layer=l, nseq=dec_batch, seg=dec_seq,
                                nseg=MIXER_ROWS // dec_seq, carry=False, emit_vrows=True,
                                name=f"mixer_sample_{l}")
        xp = _ffn(hp, *ffn_w, layer=l, final=final, name=f"ffn_prompt_{l}")
        xs = _ffn(hs, *ffn_w, layer=l, final=final, name=f"ffn_sample_{l}")
        if not final:
            win, wout = nxt

        conv_p.append(cp)
        lru_p.append(lp.reshape(batch, gw))
        conv_s.append(cs)
        lru_s.append(ls.reshape(dec_batch, gw))
        vrows_s.append(vs.reshape(dec_batch, dec_seq, gw))

    return (xp.reshape(batch, seq, D_MODEL), xs.reshape(dec_batch, dec_seq, D_MODEL),
            jnp.stack(conv_p), jnp.stack(lru_p), jnp.stack(conv_s), jnp.stack(lru_s),
            jnp.stack(vrows_s))
```

```python
import functools
import math

import jax
import jax.numpy as jnp
from jax import lax
from jax.experimental import pallas as pl
from jax.experimental.pallas import tpu as pltpu

D_MODEL = 2048
GROUP_WIDTH = D_MODEL // 2
HEADS = 8
HEAD_DIM = GROUP_WIDTH // HEADS
CONV_WIDTH = 4
MLP_CHUNK = 128
LRU_C = 8.0
EPS = 1e-6

SUBLANES = 8
BF16_ROWS = 2 * SUBLANES
CARRY_ROWS = SUBLANES
CONV_STATE = CONV_WIDTH - 1
CHUNK_HEADS = 2
PAIR = CHUNK_HEADS * HEAD_DIM
NPAIR = GROUP_WIDTH // PAIR
SCAN_PARTS = 4
OUT_COLS = 512
OUT_CHUNKS = D_MODEL // OUT_COLS

MIXER_ROWS = 256
FFN_ROWS = 1024
FFN_COLS = 512
VMEM_LIMIT = 60 * 1024 * 1024
CONVERT_STEPS = 8

_CW0, _CB, _BR, _BI, _LAM, _LNG, _LNB, _GNA, _GNB = 0, 4, 5, 6, 7, 8, 9, 10, 11
_CVEC_ROWS = 16

_BF16 = jnp.bfloat16
_F32 = jnp.float32
_MIN_NORMAL = float(jnp.finfo(jnp.float32).tiny)


def _dot(a, b):
    return jnp.dot(a, b, preferred_element_type=_F32)


def _rms(x, g):
    return x * lax.rsqrt(jnp.mean(x * x, axis=-1, keepdims=True) + EPS) * g


def _gelu(x):
    c = math.sqrt(2.0 / math.pi)
    hx = 0.5 * x
    return hx + hx * jnp.tanh(x * (c + (c * 0.044715) * (x * x)))


def _sigmoid(x):
    return 1.0 / (1.0 + jnp.exp(-x))


def _rowsum(parts):
    total = jnp.sum(parts[0], axis=-1, keepdims=True)
    for p in parts[1:]:
        total = total + jnp.sum(p, axis=-1, keepdims=True)
    return total


def _convert_slab(src, dst):
    if len(dst.shape) == 2:
        dst[...] = src[...].astype(_BF16)
    else:
        width = dst.shape[2]
        for c in range(dst.shape[0]):
            dst[c] = src[:, c * width:(c + 1) * width].astype(_BF16)


def _convert_specs(casts, steps, step_of):
    in_specs, out_specs, out_shape = [], [], []
    for w, layer, width in casts:
        _, nrow, ncol = w.shape
        hold = next(h for h in (1, 2, 4, 8)
                    if nrow * h % steps == 0 and (nrow * h // steps) % BF16_ROWS == 0)
        slab = nrow * hold // steps
        index = lambda *g, hold=hold: step_of(*g) // hold
        in_specs.append(pl.BlockSpec((None, slab, ncol),
                                     lambda *g, layer=layer, index=index: (layer, index(*g), 0)))
        if width is None:
            out_shape.append(jax.ShapeDtypeStruct((nrow, ncol), _BF16))
            out_specs.append(pl.BlockSpec((slab, ncol), lambda *g, index=index: (index(*g), 0)))
        else:
            out_shape.append(jax.ShapeDtypeStruct((ncol // width, nrow, width), _BF16))
            out_specs.append(pl.BlockSpec((ncol // width, slab, width),
                                          lambda *g, index=index: (0, index(*g), 0)))
    return in_specs, out_specs, out_shape


def _tril_kernel(w_ref, o_ref):
    tri = (lax.broadcasted_iota(jnp.int32, w_ref.shape, 1)
           >= lax.broadcasted_iota(jnp.int32, w_ref.shape, 2))
    o_ref[...] = jnp.where(tri, w_ref[...], 0.0).astype(_BF16)


def _tril(w_spatial):
    depth = w_spatial.shape[0]
    block = (None,) + tuple(w_spatial.shape[1:])
    return pl.pallas_call(
        _tril_kernel, grid=(depth,),
        in_specs=[pl.BlockSpec(block, lambda l: (l, 0, 0, 0))],
        out_specs=pl.BlockSpec(block, lambda l: (l, 0, 0, 0)),
        out_shape=jax.ShapeDtypeStruct(w_spatial.shape, _BF16),
        name="tril_spatial",
    )(w_spatial)


def _convert_kernel(*refs):
    half = len(refs) // 2
    for src, dst in zip(refs[:half], refs[half:]):
        _convert_slab(src, dst)


def _convert(casts, steps, name):
    in_specs, out_specs, out_shape = _convert_specs(casts, steps, lambda i: i)
    return pl.pallas_call(
        _convert_kernel, grid=(steps,), in_specs=in_specs, out_specs=out_specs, out_shape=out_shape,
        compiler_params=pltpu.CompilerParams(
            dimension_semantics=("arbitrary",), vmem_limit_bytes=VMEM_LIMIT),
        name=name,
    )(*[w for w, _, _ in casts])


def _mixer_kernel(*refs, seg, nseg, carry, emit_vrows, ncast):
    it = iter(refs)
    take = lambda n: [next(it) for _ in range(n)]
    x_ref = take(1)[0]
    conv0_ref, h0_ref = (None, None) if carry else take(2)
    norm1_ref, cvec_ref = take(2)
    win_ref, wgate_ref, wsp_ref, bsb_ref, wout_ref = take(5)
    cast_src = take(ncast)
    y_ref, convo_ref, ho_ref = take(3)
    vrows_ref = take(1)[0] if emit_vrows else None
    cast_dst = take(ncast)
    xpad, a3, b3, h3, hcar = take(5)
    rows = seg * nseg
    gw = GROUP_WIDTH
    groups_per_seg = seg // SUBLANES

    def vec(k, cols=slice(None)):
        return cvec_ref[k:k + 1, cols]

    def pair_cols(p):
        return slice(p * PAIR, (p + 1) * PAIR)

    state_rows = slice(CARRY_ROWS - CONV_STATE, CARRY_ROWS)
    if carry:
        t = pl.program_id(1)

        @pl.when(t == 0)
        def _():
            xpad[0, state_rows, :] = jnp.zeros((CONV_STATE, gw), _F32)
            hcar[...] = jnp.zeros_like(hcar)

        @pl.when(t > 0)
        def _():
            xpad[0, state_rows, :] = xpad[0, seg + CARRY_ROWS - CONV_STATE:seg + CARRY_ROWS, :]
    else:
        for s in range(nseg):
            xpad[s, state_rows, :] = conv0_ref[s]

    x = x_ref[...]
    xnb = _rms(x, norm1_ref[...]).astype(_BF16)

    def proj(group, p):
        return _dot(xnb, win_ref[group * NPAIR + p])

    def conv(p, xa_p):
        cols = pair_cols(p)
        for s in range(nseg):
            xpad[s, CARRY_ROWS:CARRY_ROWS + seg, cols] = xa_p[s * seg:(s + 1) * seg]
            convo_ref[s, :, cols] = xa_p[(s + 1) * seg - CONV_STATE:(s + 1) * seg]

        def shifted(k):
            parts = [xpad[s, CARRY_ROWS - k:CARRY_ROWS - k + seg, cols] for s in range(nseg)]
            return parts[0] if nseg == 1 else jnp.concatenate(parts, axis=0)

        return (shifted(3) * vec(_CW0, cols) + shifted(2) * vec(_CW0 + 1, cols)
                + shifted(1) * vec(_CW0 + 2, cols) + xa_p * vec(_CW0 + 3, cols) + vec(_CB, cols))

    def gates(p, xc_p):
        g = [_dot(xc_p[:, hh * HEAD_DIM:(hh + 1) * HEAD_DIM].astype(_BF16), wgate_ref[CHUNK_HEADS * p + hh])
             for hh in range(CHUNK_HEADS)]
        return (jnp.concatenate([gh[:, :HEAD_DIM] for gh in g], axis=1),
                jnp.concatenate([gh[:, HEAD_DIM:] for gh in g], axis=1))

    neg_lam = -vec(_LAM)
    softplus = jnp.maximum(neg_lam, 0.0) + jnp.log1p(jnp.exp(-jnp.abs(neg_lam)))
    neg_c_softplus = -LRU_C * softplus

    def lru_coeffs(p, xc_p, g_r, g_i):
        cols = pair_cols(p)
        r = _sigmoid(g_r + vec(_BR, cols))
        ig = _sigmoid(g_i + vec(_BI, cols))
        log_a = r * neg_c_softplus[:, cols]
        a = jnp.exp(log_a)
        m = -jnp.tanh(log_a) * (a * a + 1.0)
        bterm = (m * lax.rsqrt(jnp.maximum(m, _MIN_NORMAL))) * (ig * xc_p)
        for hh in range(CHUNK_HEADS):
            c = CHUNK_HEADS * p + hh
            sub = slice(hh * HEAD_DIM, (hh + 1) * HEAD_DIM)
            a3[:, c * SUBLANES:(c + 1) * SUBLANES, :] = a[:, sub].reshape(rows // SUBLANES, SUBLANES, HEAD_DIM)
            b3[:, c * SUBLANES:(c + 1) * SUBLANES, :] = bterm[:, sub].reshape(rows // SUBLANES, SUBLANES, HEAD_DIM)

    def scan_part(q, hs):
        per = groups_per_seg // SCAN_PARTS
        hs = list(hs)
        for j in range(q * per, (q + 1) * per):
            for rr in range(SUBLANES):
                for s in range(nseg):
                    jj = s * groups_per_seg + j
                    at = a3[jj, pl.ds(rr, HEADS, stride=SUBLANES), :]
                    bt = b3[jj, pl.ds(rr, HEADS, stride=SUBLANES), :]
                    hs[s] = at * hs[s] + bt
                    h3[jj, pl.ds(rr, HEADS, stride=SUBLANES), :] = hs[s]
        return hs

    def y_lru_pair(p):
        return jnp.concatenate(
            [h3[:, c * SUBLANES:(c + 1) * SUBLANES, :].reshape(rows, HEAD_DIM)
             for c in range(CHUNK_HEADS * p, CHUNK_HEADS * (p + 1))],
            axis=1)

    def layernorm_v(vg):
        mu = _rowsum(vg) * (1.0 / gw)
        vc = [g - mu for g in vg]
        rstd = lax.rsqrt(_rowsum([c * c for c in vc]) * (1.0 / gw) + EPS)
        return [vc[p] * rstd * vec(_LNG, pair_cols(p)) + vec(_LNB, pair_cols(p)) for p in range(NPAIR)]

    chunk = min(seg, MLP_CHUNK)
    nchunk = rows // chunk

    def token_mlp(h, vb, gu):
        p, hh = divmod(h, CHUNK_HEADS)
        sub = slice(hh * HEAD_DIM, (hh + 1) * HEAD_DIM)
        vh = jnp.concatenate([vb[p][k * chunk:(k + 1) * chunk, sub] for k in range(nchunk)], axis=1)
        m = _dot(wsp_ref[h, 0:chunk, 0:chunk], vh)
        bias = bsb_ref[h][:chunk]
        mixed = jnp.concatenate(
            [m[:, k * HEAD_DIM:(k + 1) * HEAD_DIM] + bias for k in range(nchunk)], axis=0)
        return gu[p][:, sub] * mixed

    def convert_weights():
        for src, dst in zip(cast_src, cast_dst):
            _convert_slab(src, dst)

    xa0 = proj(0, 0)
    xa1 = proj(0, 1)
    xc0 = conv(0, xa0)
    xa2 = proj(0, 2)
    xc1 = conv(1, xa1)
    gt0 = gates(0, xc0)
    xa3 = proj(0, 3)
    xc2 = conv(2, xa2)
    gt1 = gates(1, xc1)
    v = [None] * NPAIR
    v[0] = proj(3, 0)
    xc3 = conv(3, xa3)
    gt2 = gates(2, xc2)
    v[1] = proj(3, 1)
    lru_coeffs(0, xc0, *gt0)
    gt3 = gates(3, xc3)
    v[2] = proj(3, 2)
    lru_coeffs(1, xc1, *gt1)
    v[3] = proj(3, 3)
    lru_coeffs(2, xc2, *gt2)
    u = [None] * NPAIR
    u[0] = proj(2, 0)
    lru_coeffs(3, xc3, *gt3)

    hs = [hcar[...] if carry else h0_ref[s] for s in range(nseg)]
    u[1] = proj(2, 1)
    hs = scan_part(0, hs)
    vg = [_gelu(v[0]), _gelu(v[1])]
    u[2] = proj(2, 2)
    convert_weights()
    hs = scan_part(1, hs)
    vg += [_gelu(v[2]), _gelu(v[3])]
    u[3] = proj(2, 3)
    hs = scan_part(2, hs)
    v_n = layernorm_v(vg)
    if emit_vrows:
        for p in range(NPAIR):
            vrows_ref[:, pair_cols(p)] = v_n[p]
    vb = [n.astype(_BF16) for n in v_n]
    ga = [None] * NPAIR
    ga[0] = proj(1, 0)
    hs = scan_part(3, hs)
    if carry:
        hcar[...] = hs[0]
    for s in range(nseg):
        ho_ref[s] = hs[s]
    gu = [_gelu(u[p]) for p in range(NPAIR)]
    ga[1] = proj(1, 1)
    out_b = [token_mlp(h, vb, gu) for h in range(HEADS // 2)]
    ga[2] = proj(1, 2)
    out_b += [token_mlp(h, vb, gu) for h in range(HEADS // 2, HEADS)]
    gga = [_gelu(ga[0]), _gelu(ga[1])]
    ga[3] = proj(1, 3)
    rs_b = lax.rsqrt(_rowsum([o * o for o in out_b]) * (1.0 / gw) + EPS)
    nb = jnp.concatenate(
        [out_b[h] * rs_b * vec(_GNB, slice(h * HEAD_DIM, (h + 1) * HEAD_DIM)) for h in range(HEADS)],
        axis=1).astype(_BF16)
    gga.append(_gelu(ga[2]))

    acc = [_dot(nb, wout_ref[n, gw:2 * gw, :]) for n in range(OUT_CHUNKS)]
    gga.append(_gelu(ga[3]))
    out_a = [y_lru_pair(p) * gga[p] for p in range(NPAIR)]
    rs_a = lax.rsqrt(_rowsum([o * o for o in out_a]) * (1.0 / gw) + EPS)
    na = jnp.concatenate([out_a[p] * rs_a * vec(_GNA, pair_cols(p)) for p in range(NPAIR)],
                         axis=1).astype(_BF16)
    for n in range(OUT_CHUNKS):
        cols = slice(n * OUT_COLS, (n + 1) * OUT_COLS)
        y_ref[:, cols] = x[:, cols] + acc[n] + _dot(na, wout_ref[n, 0:gw, :])


def _layer_spec(shape, layer):
    zeros = (0,) * (len(shape) - 1)
    return pl.BlockSpec((None,) + tuple(shape[1:]), lambda *_: (layer,) + zeros,
                        pipeline_mode=pl.Buffered(1))


def _whole_spec(shape):
    zeros = (0,) * len(shape)
    return pl.BlockSpec(tuple(shape), lambda *_: zeros, pipeline_mode=pl.Buffered(1))


def _mixer(x2d, state, norm1, cvec, win, wgate, wsp, bsb, wout, casts=(), *, layer, nseq, seg, nseg,
           carry, emit_vrows, name):
    rows = seg * nseg
    total = x2d.shape[0]
    gw = GROUP_WIDTH
    if carry:
        steps = total // nseq // rows
        grid = (nseq, steps)
        row_map = lambda b, t: (b * steps + t, 0)
        seq_map = lambda b, t: (b, 0, 0)
        state_specs = []
        cast_in, cast_out, cast_shape = _convert_specs(casts, nseq * steps, lambda b, t: b * steps + t)
    else:
        assert not casts
        grid = (total // rows,)
        row_map = lambda i: (i, 0)
        seq_map = lambda i: (i, 0, 0)
        state_map = lambda i: (layer, i, 0, 0)
        state_specs = [pl.BlockSpec((None, nseg, CONV_STATE, gw), state_map),
                       pl.BlockSpec((None, nseg, HEADS, HEAD_DIM), state_map)]
        cast_in, cast_out, cast_shape = [], [], []
    out_shape = [jax.ShapeDtypeStruct((total, D_MODEL), _F32),
                 jax.ShapeDtypeStruct((nseq, CONV_STATE, gw), _F32),
                 jax.ShapeDtypeStruct((nseq, HEADS, HEAD_DIM), _F32)]
    out_specs = [pl.BlockSpec((rows, D_MODEL), row_map),
                 pl.BlockSpec((nseg, CONV_STATE, gw), seq_map),
                 pl.BlockSpec((nseg, HEADS, HEAD_DIM), seq_map)]
    if emit_vrows:
        out_shape.append(jax.ShapeDtypeStruct((total, gw), _F32))
        out_specs.append(pl.BlockSpec((rows, gw), row_map))
    scan_shape = (rows // SUBLANES, HEADS * SUBLANES, HEAD_DIM)
    return pl.pallas_call(
        functools.partial(_mixer_kernel, seg=seg, nseg=nseg, carry=carry, emit_vrows=emit_vrows,
                          ncast=len(casts)),
        grid=grid,
        in_specs=[pl.BlockSpec((rows, D_MODEL), row_map), *state_specs,
                  _layer_spec(norm1.shape, layer), _layer_spec(cvec.shape, layer),
                  _whole_spec(win.shape), _layer_spec(wgate.shape, layer),
                  _layer_spec(wsp.shape, layer), _layer_spec(bsb.shape, layer),
                  _whole_spec(wout.shape), *cast_in],
        out_specs=out_specs + cast_out,
        out_shape=out_shape + cast_shape,
        scratch_shapes=[pltpu.VMEM((nseg, seg + CARRY_ROWS, gw), _F32),
                        pltpu.VMEM(scan_shape, _F32), pltpu.VMEM(scan_shape, _F32),
                        pltpu.VMEM(scan_shape, _F32), pltpu.VMEM((HEADS, HEAD_DIM), _F32)],
        compiler_params=pltpu.CompilerParams(
            dimension_semantics=("arbitrary",) * len(grid), vmem_limit_bytes=VMEM_LIMIT),
        name=name,
    )(x2d, *state, norm1, cvec, win, wgate, wsp, bsb, wout, *[w for w, _, _ in casts])


def _ffn_kernel(h_ref, norm2_ref, wg_ref, wu_ref, *rest, final):
    wd_refs, (normf_ref, o_ref, hn_ref) = rest[:OUT_CHUNKS], rest[OUT_CHUNKS:]
    j = pl.program_id(1)
    last = pl.num_programs(1) - 1

    def add_slab(base_ref, normalise=False):
        hn = hn_ref[...]
        g = _dot(hn, wg_ref[...])
        u = _dot(hn, wu_ref[...])
        act = ((g * _sigmoid(g)) * u).astype(_BF16)
        chunks = [slice(n * OUT_COLS, (n + 1) * OUT_COLS) for n in range(OUT_CHUNKS)]
        outs = [base_ref[:, cols] + _dot(act, wd_refs[n][...]) for n, cols in enumerate(chunks)]
        if normalise:
            rs = lax.rsqrt(_rowsum([o * o for o in outs]) * (1.0 / D_MODEL) + EPS)
            outs = [o * rs * normf_ref[:, cols] for o, cols in zip(outs, chunks)]
        for o, cols in zip(outs, chunks):
            o_ref[:, cols] = o

    @pl.when(j == 0)
    def _():
        hn_ref[...] = _rms(h_ref[...], norm2_ref[...]).astype(_BF16)
        add_slab(h_ref)

    @pl.when((j > 0) & (j < last) if final else (j > 0))
    def _():
        add_slab(o_ref)

    if final:
        @pl.when(j == last)
        def _():
            add_slab(o_ref, normalise=True)


def _ffn(h2d, norm2, wg, wu, wd, normf, *, layer, final, name):
    total = h2d.shape[0]
    d_ff = wg.shape[-1]
    rows = min(FFN_ROWS, total)
    grid = (total // rows, d_ff // FFN_COLS)
    assert grid[1] >= 2, "the first and the last slab of a row tile must be different grid steps"
    return pl.pallas_call(
        functools.partial(_ffn_kernel, final=final),
        grid=grid,
        in_specs=[pl.BlockSpec((rows, D_MODEL), lambda i, j: (i, 0)),
                  _layer_spec(norm2.shape, layer),
                  pl.BlockSpec((D_MODEL, FFN_COLS), lambda i, j: (0, j)),
                  pl.BlockSpec((D_MODEL, FFN_COLS), lambda i, j: (0, j)),
                  *[pl.BlockSpec((FFN_COLS, OUT_COLS), lambda i, j, n=n: (j, n)) for n in range(OUT_CHUNKS)],
                  _layer_spec(normf.shape, 0)],
        out_specs=pl.BlockSpec((rows, D_MODEL), lambda i, j: (i, 0)),
        out_shape=jax.ShapeDtypeStruct((total, D_MODEL), _F32),
        scratch_shapes=[pltpu.VMEM((rows, D_MODEL), _BF16)],
        compiler_params=pltpu.CompilerParams(
            dimension_semantics=("arbitrary", "arbitrary"), vmem_limit_bytes=VMEM_LIMIT),
        name=name,
    )(h2d, norm2, wg, wu, *([wd] * OUT_CHUNKS), normf)


def kernel(x_prompt, x_sample, state_conv, state_lru, norm1, w_in, conv_w, conv_b, w_rgate, b_rgate,
           w_igate, b_igate, lru_param, v_ln_g, v_ln_b, w_spatial, b_spatial, gn_a, gn_b, w_out,
           norm2, w_gate, w_up, w_down, norm_f):
    depth = w_in.shape[0]
    batch, seq, _ = x_prompt.shape
    dec_batch, dec_seq, _ = x_sample.shape
    gw = GROUP_WIDTH

    xp = x_prompt.reshape(batch * seq, D_MODEL)
    xs = x_sample.reshape(dec_batch * dec_seq, D_MODEL)
    sample_state = (state_conv, state_lru.reshape(depth, dec_batch, HEADS, HEAD_DIM))

    row = lambda a: a[:, None, :]
    cvec = jnp.concatenate(
        [conv_w, row(conv_b), row(b_rgate), row(b_igate), row(lru_param), row(v_ln_g), row(v_ln_b),
         row(gn_a), row(gn_b), jnp.zeros((depth, _CVEC_ROWS - 12, gw), _F32)], axis=1)
    wgate = jnp.concatenate([w_rgate, w_igate], axis=-1).astype(_BF16)
    bsb = jnp.broadcast_to(b_spatial[..., None], (depth, HEADS, MLP_CHUNK, HEAD_DIM))
    normf = norm_f.reshape(1, 1, D_MODEL)
    steps = seq // MIXER_ROWS

    def proj_casts(l):
        return [(w_in, l, PAIR), (w_out, l, OUT_COLS)]

    win, wout = _convert(proj_casts(0), CONVERT_STEPS, name="convert_proj_0")
    wsp = _tril(w_spatial)

    conv_p, lru_p, conv_s, lru_s, vrows_s = [], [], [], [], []
    for l in range(depth):
        final = l == depth - 1
        mixer_w = (norm1[:, None, :], cvec, win, wgate, wsp, bsb, wout)
        casts = [(w_gate, l, None), (w_up, l, None), (w_down, l, None)]
        if not final:
            casts += proj_casts(l + 1)
        hp, cp, lp, wg, wu, wd, *nxt = _mixer(xp, (), *mixer_w, casts, layer=l, nseq=batch,
                                              seg=MIXER_ROWS, nseg=1, carry=True, emit_vrows=False,
                                              name=f"mixer_prompt_{l}")
        ffn_w = (norm2[:, None, :], wg, wu, wd, normf)
        hs, cs, ls, vs = _mixer(xs, sample_state, *mixer_w, layer=l, nseq=dec_batch, seg=dec_seq,
                                nseg=MIXER_ROWS // dec_seq, carry=False, emit_vrows=True,
                                name=f"mixer_sample_{l}")
        xp = _ffn(hp, *ffn_w, layer=l, final=final, name=f"ffn_prompt_{l}")
        xs = _ffn(hs, *ffn_w, layer=l, final=final, name=f"ffn_sample_{l}")
        if not final:
            win, wout = nxt

        conv_p.append(cp)
        lru_p.append(lp.reshape(batch, gw))
        conv_s.append(cs)
        lru_s.append(ls.reshape(dec_batch, gw))
        vrows_s.append(vs.reshape(dec_batch, dec_seq, gw))

    return (xp.reshape(batch, seq, D_MODEL), xs.reshape(dec_batch, dec_seq, D_MODEL),
            jnp.stack(conv_p), jnp.stack(lru_p), jnp.stack(conv_s), jnp.stack(lru_s),
            jnp.stack(vrows_s))
```

```python
import functools
import math

import jax
import jax.numpy as jnp
from jax import lax
from jax.experimental import pallas as pl
from jax.experimental.pallas import tpu as pltpu

D_MODEL = 2048
GROUP_WIDTH = D_MODEL // 2
HEADS = 8
HEAD_DIM = GROUP_WIDTH // HEADS
CONV_WIDTH = 4
MLP_CHUNK = 128
LRU_C = 8.0
EPS = 1e-6

SUBLANES = 8
BF16_ROWS = 2 * SUBLANES
CARRY_ROWS = SUBLANES
CONV_STATE = CONV_WIDTH - 1
CHUNK_HEADS = 2
PAIR = CHUNK_HEADS * HEAD_DIM
NPAIR = GROUP_WIDTH // PAIR
SCAN_PARTS = 4
OUT_COLS = 512
OUT_CHUNKS = D_MODEL // OUT_COLS

MIXER_ROWS = 256
FFN_ROWS = 1024
FFN_COLS = 512
VMEM_LIMIT = 60 * 1024 * 1024
CONVERT_STEPS = 8

_CW0, _CB, _BR, _BI, _LAM, _LNG, _LNB, _GNA, _GNB = 0, 4, 5, 6, 7, 8, 9, 10, 11
_CVEC_ROWS = 16

_BF16 = jnp.bfloat16
_F32 = jnp.float32
_MIN_NORMAL = float(jnp.finfo(jnp.float32).tiny)


def _dot(a, b):
    return jnp.dot(a, b, preferred_element_type=_F32)


def _rms(x, g):
    return x * lax.rsqrt(jnp.mean(x * x, axis=-1, keepdims=True) + EPS) * g


def _gelu(x):
    c = math.sqrt(2.0 / math.pi)
    hx = 0.5 * x
    return hx + hx * jnp.tanh(x * (c + (c * 0.044715) * (x * x)))


def _sigmoid(x):
    return 1.0 / (1.0 + jnp.exp(-x))


def _rowsum(parts):
    total = jnp.sum(parts[0], axis=-1, keepdims=True)
    for p in parts[1:]:
        total = total + jnp.sum(p, axis=-1, keepdims=True)
    return total


def _convert_slab(src, dst):
    if len(dst.shape) == 2:
        dst[...] = src[...].astype(_BF16)
    else:
        width = dst.shape[2]
        for c in range(dst.shape[0]):
            dst[c] = src[:, c * width:(c + 1) * width].astype(_BF16)


def _convert_specs(casts, steps, step_of):
    in_specs, out_specs, out_shape = [], [], []
    for w, layer, width in casts:
        _, nrow, ncol = w.shape
        hold = next(h for h in (1, 2, 4, 8)
                    if nrow * h % steps == 0 and (nrow * h // steps) % BF16_ROWS == 0)
        slab = nrow * hold // steps
        index = lambda *g, hold=hold: step_of(*g) // hold
        in_specs.append(pl.BlockSpec((None, slab, ncol),
                                     lambda *g, layer=layer, index=index: (layer, index(*g), 0)))
        if width is None:
            out_shape.append(jax.ShapeDtypeStruct((nrow, ncol), _BF16))
            out_specs.append(pl.BlockSpec((slab, ncol), lambda *g, index=index: (index(*g), 0)))
        else:
            out_shape.append(jax.ShapeDtypeStruct((ncol // width, nrow, width), _BF16))
            out_specs.append(pl.BlockSpec((ncol // width, slab, width),
                                          lambda *g, index=index: (0, index(*g), 0)))
    return in_specs, out_specs, out_shape


def _tril_kernel(w_ref, o_ref):
    tri = (lax.broadcasted_iota(jnp.int32, w_ref.shape, 1)
           >= lax.broadcasted_iota(jnp.int32, w_ref.shape, 2))
    o_ref[...] = jnp.where(tri, w_ref[...], 0.0).astype(_BF16)


def _tril(w_spatial):
    depth = w_spatial.shape[0]
    block = (None,) + tuple(w_spatial.shape[1:])
    return pl.pallas_call(
        _tril_kernel, grid=(depth,),
        in_specs=[pl.BlockSpec(block, lambda l: (l, 0, 0, 0))],
        out_specs=pl.BlockSpec(block, lambda l: (l, 0, 0, 0)),
        out_shape=jax.ShapeDtypeStruct(w_spatial.shape, _BF16),
        name="tril_spatial",
    )(w_spatial)


def _convert_kernel(*refs):
    half = len(refs) // 2
    for src, dst in zip(refs[:half], refs[half:]):
        _convert_slab(src, dst)


def _convert(casts, steps, name):
    in_specs, out_specs, out_shape = _convert_specs(casts, steps, lambda i: i)
    return pl.pallas_call(
        _convert_kernel, grid=(steps,), in_specs=in_specs, out_specs=out_specs, out_shape=out_shape,
        compiler_params=pltpu.CompilerParams(
            dimension_semantics=("arbitrary",), vmem_limit_bytes=VMEM_LIMIT),
        name=name,
    )(*[w for w, _, _ in casts])


def _mixer_kernel(*refs, seg, nseg, carry, emit_vrows, ncast):
    it = iter(refs)
    take = lambda n: [next(it) for _ in range(n)]
    x_ref = take(1)[0]
    conv0_ref, h0_ref = (None, None) if carry else take(2)
    norm1_ref, cvec_ref = take(2)
    win_ref, wgate_ref, wsp_ref, bsb_ref, wout_hbm = take(5)
    cast_src = take(ncast)
    y_ref, convo_ref, ho_ref = take(3)
    vrows_ref = take(1)[0] if emit_vrows else None
    cast_dst = take(ncast)
    xpad, a3, b3, h3, hcar, wout_ref, wout_sem = take(7)
    rows = seg * nseg
    gw = GROUP_WIDTH
    groups_per_seg = seg // SUBLANES

    def vec(k, cols=slice(None)):
        return cvec_ref[k:k + 1, cols]

    def pair_cols(p):
        return slice(p * PAIR, (p + 1) * PAIR)

    first_step = (pl.program_id(0) == 0) & (pl.program_id(1) == 0) if carry else pl.program_id(0) == 0
    wout_copy = pltpu.make_async_copy(wout_hbm, wout_ref, wout_sem)

    @pl.when(first_step)
    def _():
        wout_copy.start()

    state_rows = slice(CARRY_ROWS - CONV_STATE, CARRY_ROWS)
    if carry:
        t = pl.program_id(1)

        @pl.when(t == 0)
        def _():
            xpad[0, state_rows, :] = jnp.zeros((CONV_STATE, gw), _F32)
            hcar[...] = jnp.zeros_like(hcar)

        @pl.when(t > 0)
        def _():
            xpad[0, state_rows, :] = xpad[0, seg + CARRY_ROWS - CONV_STATE:seg + CARRY_ROWS, :]
    else:
        for s in range(nseg):
            xpad[s, state_rows, :] = conv0_ref[s]

    x = x_ref[...]
    xnb = _rms(x, norm1_ref[...]).astype(_BF16)

    def proj(group, p):
        return _dot(xnb, win_ref[group * NPAIR + p])

    def conv(p, xa_p):
        cols = pair_cols(p)
        for s in range(nseg):
            xpad[s, CARRY_ROWS:CARRY_ROWS + seg, cols] = xa_p[s * seg:(s + 1) * seg]
            convo_ref[s, :, cols] = xa_p[(s + 1) * seg - CONV_STATE:(s + 1) * seg]

        def shifted(k):
            parts = [xpad[s, CARRY_ROWS - k:CARRY_ROWS - k + seg, cols] for s in range(nseg)]
            return parts[0] if nseg == 1 else jnp.concatenate(parts, axis=0)

        return (shifted(3) * vec(_CW0, cols) + shifted(2) * vec(_CW0 + 1, cols)
                + shifted(1) * vec(_CW0 + 2, cols) + xa_p * vec(_CW0 + 3, cols) + vec(_CB, cols))

    def gates(p, xc_p):
        g = [_dot(xc_p[:, hh * HEAD_DIM:(hh + 1) * HEAD_DIM].astype(_BF16), wgate_ref[CHUNK_HEADS * p + hh])
             for hh in range(CHUNK_HEADS)]
        return (jnp.concatenate([gh[:, :HEAD_DIM] for gh in g], axis=1),
                jnp.concatenate([gh[:, HEAD_DIM:] for gh in g], axis=1))

    neg_lam = -vec(_LAM)
    softplus = jnp.maximum(neg_lam, 0.0) + jnp.log1p(jnp.exp(-jnp.abs(neg_lam)))
    neg_c_softplus = -LRU_C * softplus

    def lru_coeffs(p, xc_p, g_r, g_i):
        cols = pair_cols(p)
        r = _sigmoid(g_r + vec(_BR, cols))
        ig = _sigmoid(g_i + vec(_BI, cols))
        log_a = r * neg_c_softplus[:, cols]
        a = jnp.exp(log_a)
        m = -jnp.tanh(log_a) * (a * a + 1.0)
        bterm = (m * lax.rsqrt(jnp.maximum(m, _MIN_NORMAL))) * (ig * xc_p)
        for hh in range(CHUNK_HEADS):
            c = CHUNK_HEADS * p + hh
            sub = slice(hh * HEAD_DIM, (hh + 1) * HEAD_DIM)
            a3[:, c * SUBLANES:(c + 1) * SUBLANES, :] = a[:, sub].reshape(rows // SUBLANES, SUBLANES, HEAD_DIM)
            b3[:, c * SUBLANES:(c + 1) * SUBLANES, :] = bterm[:, sub].reshape(rows // SUBLANES, SUBLANES, HEAD_DIM)

    def scan_part(q, hs):
        per = groups_per_seg // SCAN_PARTS
        hs = list(hs)
        for j in range(q * per, (q + 1) * per):
            for rr in range(SUBLANES):
                for s in range(nseg):
                    jj = s * groups_per_seg + j
                    at = a3[jj, pl.ds(rr, HEADS, stride=SUBLANES), :]
                    bt = b3[jj, pl.ds(rr, HEADS, stride=SUBLANES), :]
                    hs[s] = at * hs[s] + bt
                    h3[jj, pl.ds(rr, HEADS, stride=SUBLANES), :] = hs[s]
        return hs

    def y_lru_pair(p):
        return jnp.concatenate(
            [h3[:, c * SUBLANES:(c + 1) * SUBLANES, :].reshape(rows, HEAD_DIM)
             for c in range(CHUNK_HEADS * p, CHUNK_HEADS * (p + 1))],
            axis=1)

    def layernorm_v(vg):
        mu = _rowsum(vg) * (1.0 / gw)
        vc = [g - mu for g in vg]
        rstd = lax.rsqrt(_rowsum([c * c for c in vc]) * (1.0 / gw) + EPS)
        return [vc[p] * rstd * vec(_LNG, pair_cols(p)) + vec(_LNB, pair_cols(p)) for p in range(NPAIR)]

    chunk = min(seg, MLP_CHUNK)
    nchunk = rows // chunk

    def token_mlp(h, vb, gu):
        p, hh = divmod(h, CHUNK_HEADS)
        sub = slice(hh * HEAD_DIM, (hh + 1) * HEAD_DIM)
        vh = jnp.concatenate([vb[p][k * chunk:(k + 1) * chunk, sub] for k in range(nchunk)], axis=1)
        m = _dot(wsp_ref[h, 0:chunk, 0:chunk], vh)
        bias = bsb_ref[h][:chunk]
        mixed = jnp.concatenate(
            [m[:, k * HEAD_DIM:(k + 1) * HEAD_DIM] + bias for k in range(nchunk)], axis=0)
        return gu[p][:, sub] * mixed

    def convert_weights():
        for src, dst in zip(cast_src, cast_dst):
            _convert_slab(src, dst)

    xa0 = proj(0, 0)
    xa1 = proj(0, 1)
    xc0 = conv(0, xa0)
    xa2 = proj(0, 2)
    xc1 = conv(1, xa1)
    gt0 = gates(0, xc0)
    xa3 = proj(0, 3)
    xc2 = conv(2, xa2)
    gt1 = gates(1, xc1)
    v = [None] * NPAIR
    v[0] = proj(3, 0)
    xc3 = conv(3, xa3)
    gt2 = gates(2, xc2)
    v[1] = proj(3, 1)
    lru_coeffs(0, xc0, *gt0)
    gt3 = gates(3, xc3)
    v[2] = proj(3, 2)
    lru_coeffs(1, xc1, *gt1)
    v[3] = proj(3, 3)
    lru_coeffs(2, xc2, *gt2)
    u = [None] * NPAIR
    u[0] = proj(2, 0)
    lru_coeffs(3, xc3, *gt3)

    hs = [hcar[...] if carry else h0_ref[s] for s in range(nseg)]
    u[1] = proj(2, 1)
    hs = scan_part(0, hs)
    vg = [_gelu(v[0]), _gelu(v[1])]
    u[2] = proj(2, 2)
    convert_weights()
    hs = scan_part(1, hs)
    vg += [_gelu(v[2]), _gelu(v[3])]
    u[3] = proj(2, 3)
    hs = scan_part(2, hs)
    v_n = layernorm_v(vg)
    if emit_vrows:
        for p in range(NPAIR):
            vrows_ref[:, pair_cols(p)] = v_n[p]
    vb = [n.astype(_BF16) for n in v_n]
    ga = [None] * NPAIR
    ga[0] = proj(1, 0)
    hs = scan_part(3, hs)
    if carry:
        hcar[...] = hs[0]
    for s in range(nseg):
        ho_ref[s] = hs[s]
    gu = [_gelu(u[p]) for p in range(NPAIR)]
    ga[1] = proj(1, 1)
    out_b = [token_mlp(h, vb, gu) for h in range(HEADS // 2)]
    ga[2] = proj(1, 2)
    out_b += [token_mlp(h, vb, gu) for h in range(HEADS // 2, HEADS)]
    gga = [_gelu(ga[0]), _gelu(ga[1])]
    ga[3] = proj(1, 3)
    rs_b = lax.rsqrt(_rowsum([o * o for o in out_b]) * (1.0 / gw) + EPS)
    nb = jnp.concatenate(
        [out_b[h] * rs_b * vec(_GNB, slice(h * HEAD_DIM, (h + 1) * HEAD_DIM)) for h in range(HEADS)],
        axis=1).astype(_BF16)
    gga.append(_gelu(ga[2]))

    @pl.when(first_step)
    def _():
        wout_copy.wait()

    acc = [_dot(nb, wout_ref[n, gw:2 * gw, :]) for n in range(OUT_CHUNKS)]
    gga.append(_gelu(ga[3]))
    out_a = [y_lru_pair(p) * gga[p] for p in range(NPAIR)]
    rs_a = lax.rsqrt(_rowsum([o * o for o in out_a]) * (1.0 / gw) + EPS)
    na = jnp.concatenate([out_a[p] * rs_a * vec(_GNA, pair_cols(p)) for p in range(NPAIR)],
                         axis=1).astype(_BF16)
    for n in range(OUT_CHUNKS):
        cols = slice(n * OUT_COLS, (n + 1) * OUT_COLS)
        y_ref[:, cols] = x[:, cols] + acc[n] + _dot(na, wout_ref[n, 0:gw, :])


def _layer_spec(shape, layer):
    zeros = (0,) * (len(shape) - 1)
    return pl.BlockSpec((None,) + tuple(shape[1:]), lambda *_: (layer,) + zeros,
                        pipeline_mode=pl.Buffered(1))


def _whole_spec(shape):
    zeros = (0,) * len(shape)
    return pl.BlockSpec(tuple(shape), lambda *_: zeros, pipeline_mode=pl.Buffered(1))


def _mixer(x2d, state, norm1, cvec, win, wgate, wsp, bsb, wout, casts=(), *, layer, nseq, seg, nseg,
           carry, emit_vrows, name):
    rows = seg * nseg
    total = x2d.shape[0]
    gw = GROUP_WIDTH
    if carry:
        steps = total // nseq // rows
        grid = (nseq, steps)
        row_map = lambda b, t: (b * steps + t, 0)
        seq_map = lambda b, t: (b, 0, 0)
        state_specs = []
        cast_in, cast_out, cast_shape = _convert_specs(casts, nseq * steps, lambda b, t: b * steps + t)
    else:
        assert not casts
        grid = (total // rows,)
        row_map = lambda i: (i, 0)
        seq_map = lambda i: (i, 0, 0)
        state_map = lambda i: (layer, i, 0, 0)
        state_specs = [pl.BlockSpec((None, nseg, CONV_STATE, gw), state_map),
                       pl.BlockSpec((None, nseg, HEADS, HEAD_DIM), state_map)]
        cast_in, cast_out, cast_shape = [], [], []
    out_shape = [jax.ShapeDtypeStruct((total, D_MODEL), _F32),
                 jax.ShapeDtypeStruct((nseq, CONV_STATE, gw), _F32),
                 jax.ShapeDtypeStruct((nseq, HEADS, HEAD_DIM), _F32)]
    out_specs = [pl.BlockSpec((rows, D_MODEL), row_map),
                 pl.BlockSpec((nseg, CONV_STATE, gw), seq_map),
                 pl.BlockSpec((nseg, HEADS, HEAD_DIM), seq_map)]
    if emit_vrows:
        out_shape.append(jax.ShapeDtypeStruct((total, gw), _F32))
        out_specs.append(pl.BlockSpec((rows, gw), row_map))
    scan_shape = (rows // SUBLANES, HEADS * SUBLANES, HEAD_DIM)
    return pl.pallas_call(
        functools.partial(_mixer_kernel, seg=seg, nseg=nseg, carry=carry, emit_vrows=emit_vrows,
                          ncast=len(casts)),
        grid=grid,
        in_specs=[pl.BlockSpec((rows, D_MODEL), row_map), *state_specs,
                  _layer_spec(norm1.shape, layer), _layer_spec(cvec.shape, layer),
                  _whole_spec(win.shape), _layer_spec(wgate.shape, layer),
                  _layer_spec(wsp.shape, layer), _layer_spec(bsb.shape, layer),
                  pl.BlockSpec(memory_space=pl.ANY), *cast_in],
        out_specs=out_specs + cast_out,
        out_shape=out_shape + cast_shape,
        scratch_shapes=[pltpu.VMEM((nseg, seg + CARRY_ROWS, gw), _F32),
                        pltpu.VMEM(scan_shape, _F32), pltpu.VMEM(scan_shape, _F32),
                        pltpu.VMEM(scan_shape, _F32), pltpu.VMEM((HEADS, HEAD_DIM), _F32),
                        pltpu.VMEM(wout.shape, _BF16), pltpu.SemaphoreType.DMA(())],
        compiler_params=pltpu.CompilerParams(
            dimension_semantics=("arbitrary",) * len(grid), vmem_limit_bytes=VMEM_LIMIT),
        name=name,
    )(x2d, *state, norm1, cvec, win, wgate, wsp, bsb, wout, *[w for w, _, _ in casts])


def _ffn_kernel(h_ref, norm2_ref, wg_ref, wu_ref, *rest, final):
    wd_refs, (normf_ref, o_ref, hn_ref) = rest[:OUT_CHUNKS], rest[OUT_CHUNKS:]
    j = pl.program_id(1)
    last = pl.num_programs(1) - 1

    def add_slab(base_ref, normalise=False):
        hn = hn_ref[...]
        g = _dot(hn, wg_ref[...])
        u = _dot(hn, wu_ref[...])
        act = ((g * _sigmoid(g)) * u).astype(_BF16)
        chunks = [slice(n * OUT_COLS, (n + 1) * OUT_COLS) for n in range(OUT_CHUNKS)]
        outs = [base_ref[:, cols] + _dot(act, wd_refs[n][...]) for n, cols in enumerate(chunks)]
        if normalise:
            rs = lax.rsqrt(_rowsum([o * o for o in outs]) * (1.0 / D_MODEL) + EPS)
            outs = [o * rs * normf_ref[:, cols] for o, cols in zip(outs, chunks)]
        for o, cols in zip(outs, chunks):
            o_ref[:, cols] = o

    @pl.when(j == 0)
    def _():
        hn_ref[...] = _rms(h_ref[...], norm2_ref[...]).astype(_BF16)
        add_slab(h_ref)

    @pl.when((j > 0) & (j < last) if final else (j > 0))
    def _():
        add_slab(o_ref)

    if final:
        @pl.when(j == last)
        def _():
            add_slab(o_ref, normalise=True)


def _ffn(h2d, norm2, wg, wu, wd, normf, *, layer, final, name):
    total = h2d.shape[0]
    d_ff = wg.shape[-1]
    rows = min(FFN_ROWS, total)
    grid = (total // rows, d_ff // FFN_COLS)
    assert grid[1] >= 2, "the first and the last slab of a row tile must be different grid steps"
    return pl.pallas_call(
        functools.partial(_ffn_kernel, final=final),
        grid=grid,
        in_specs=[pl.BlockSpec((rows, D_MODEL), lambda i, j: (i, 0)),
                  _layer_spec(norm2.shape, layer),
                  pl.BlockSpec((D_MODEL, FFN_COLS), lambda i, j: (0, j)),
                  pl.BlockSpec((D_MODEL, FFN_COLS), lambda i, j: (0, j)),
                  *[pl.BlockSpec((FFN_COLS, OUT_COLS), lambda i, j, n=n: (j, n)) for n in range(OUT_CHUNKS)],
                  _layer_spec(normf.shape, 0)],
        out_specs=pl.BlockSpec((rows, D_MODEL), lambda i, j: (i, 0)),
        out_shape=jax.ShapeDtypeStruct((total, D_MODEL), _F32),
        scratch_shapes=[pltpu.VMEM((rows, D_MODEL), _BF16)],
        compiler_params=pltpu.CompilerParams(
            dimension_semantics=("arbitrary", "arbitrary"), vmem_limit_bytes=VMEM_LIMIT),
        name=name,
    )(h2d, norm2, wg, wu, *([wd] * OUT_CHUNKS), normf)


def kernel(x_prompt, x_sample, state_conv, state_lru, norm1, w_in, conv_w, conv_b, w_rgate, b_rgate,
           w_igate, b_igate, lru_param, v_ln_g, v_ln_b, w_spatial, b_spatial, gn_a, gn_b, w_out,
           norm2, w_gate, w_up, w_down, norm_f):
    depth = w_in.shape[0]
    batch, seq, _ = x_prompt.shape
    dec_batch, dec_seq, _ = x_sample.shape
    gw = GROUP_WIDTH

    xp = x_prompt.reshape(batch * seq, D_MODEL)
    xs = x_sample.reshape(dec_batch * dec_seq, D_MODEL)
    sample_state = (state_conv, state_lru.reshape(depth, dec_batch, HEADS, HEAD_DIM))

    row = lambda a: a[:, None, :]
    cvec = jnp.concatenate(
        [conv_w, row(conv_b), row(b_rgate), row(b_igate), row(lru_param), row(v_ln_g), row(v_ln_b),
         row(gn_a), row(gn_b), jnp.zeros((depth, _CVEC_ROWS - 12, gw), _F32)], axis=1)
    wgate = jnp.concatenate([w_rgate, w_igate], axis=-1).astype(_BF16)
    bsb = jnp.broadcast_to(b_spatial[..., None], (depth, HEADS, MLP_CHUNK, HEAD_DIM))
    normf = norm_f.reshape(1, 1, D_MODEL)
    steps = seq // MIXER_ROWS

    def proj_casts(l):
        return [(w_in, l, PAIR), (w_out, l, OUT_COLS)]

    win, wout = _convert(proj_casts(0), CONVERT_STEPS, name="convert_proj_0")
    wsp = _tril(w_spatial)

    conv_p, lru_p, conv_s, lru_s, vrows_s = [], [], [], [], []
    for l in range(depth):
        final = l == depth - 1
        mixer_w = (norm1[:, None, :], cvec, win, wgate, wsp, bsb, wout)
        casts = [(w_gate, l, None), (w_up, l, None), (w_down, l, None)]
        if not final:
            casts += proj_casts(l + 1)
        hp, cp, lp, wg, wu, wd, *nxt = _mixer(xp, (), *mixer_w, casts, layer=l, nseq=batch,
                                              seg=MIXER_ROWS, nseg=1, carry=True, emit_vrows=False,
                                              name=f"mixer_prompt_{l}")
        ffn_w = (norm2[:, None, :], wg, wu, wd, normf)
        hs, cs, ls, vs = _mixer(xs, sample_state, *mixer_w, layer=l, nseq=dec_batch, seg=dec_seq,
                                nseg=MIXER_ROWS // dec_seq, carry=False, emit_vrows=True,
                                name=f"mixer_sample_{l}")
        xp = _ffn(hp, *ffn_w, layer=l, final=final, name=f"ffn_prompt_{l}")
        xs = _ffn(hs, *ffn_w, layer=l, final=final, name=f"ffn_sample_{l}")
        if not final:
            win, wout = nxt

        conv_p.append(cp)
        lru_p.append(lp.reshape(batch, gw))
        conv_s.append(cs)
        lru_s.append(ls.reshape(dec_batch, gw))
        vrows_s.append(vs.reshape(dec_batch, dec_seq, gw))

    return (xp.reshape(batch, seq, D_MODEL), xs.reshape(dec_batch, dec_seq, D_MODEL),
            jnp.stack(conv_p), jnp.stack(lru_p), jnp.stack(conv_s), jnp.stack(lru_s),
            jnp.stack(vrows_s))
```

```python
import functools
import math

import jax
import jax.numpy as jnp
from jax import lax
from jax.experimental import pallas as pl
from jax.experimental.pallas import tpu as pltpu

D_MODEL = 2048
GROUP_WIDTH = D_MODEL // 2
HEADS = 8
HEAD_DIM = GROUP_WIDTH // HEADS
CONV_WIDTH = 4
MLP_CHUNK = 128
LRU_C = 8.0
EPS = 1e-6

SUBLANES = 8
BF16_ROWS = 2 * SUBLANES
CARRY_ROWS = SUBLANES
CONV_STATE = CONV_WIDTH - 1
CHUNK_HEADS = 2
PAIR = CHUNK_HEADS * HEAD_DIM
NPAIR = GROUP_WIDTH // PAIR
SCAN_PARTS = 4
OUT_COLS = 512
OUT_CHUNKS = D_MODEL // OUT_COLS

MIXER_ROWS = 256
FFN_ROWS = 1024
FFN_COLS = 512
VMEM_LIMIT = 60 * 1024 * 1024
CONVERT_STEPS = 8

_CW0, _CB, _BR, _BI, _LAM, _LNG, _LNB, _GNA, _GNB = 0, 4, 5, 6, 7, 8, 9, 10, 11
_CVEC_ROWS = 16

_BF16 = jnp.bfloat16
_F32 = jnp.float32
_MIN_NORMAL = float(jnp.finfo(jnp.float32).tiny)


def _dot(a, b):
    return jnp.dot(a, b, preferred_element_type=_F32)


def _rms(x, g):
    return x * lax.rsqrt(jnp.mean(x * x, axis=-1, keepdims=True) + EPS) * g


def _gelu(x):
    c = math.sqrt(2.0 / math.pi)
    hx = 0.5 * x
    return hx + hx * jnp.tanh(x * (c + (c * 0.044715) * (x * x)))


def _sigmoid(x):
    return 1.0 / (1.0 + jnp.exp(-x))


def _rowsum(parts):
    total = jnp.sum(parts[0], axis=-1, keepdims=True)
    for p in parts[1:]:
        total = total + jnp.sum(p, axis=-1, keepdims=True)
    return total


def _convert_slab(src, dst):
    if len(dst.shape) == 2:
        dst[...] = src[...].astype(_BF16)
    else:
        width = dst.shape[2]
        for c in range(dst.shape[0]):
            dst[c] = src[:, c * width:(c + 1) * width].astype(_BF16)


def _convert_specs(casts, steps, step_of):
    in_specs, out_specs, out_shape = [], [], []
    for w, layer, width in casts:
        _, nrow, ncol = w.shape
        hold = next(h for h in (1, 2, 4, 8)
                    if nrow * h % steps == 0 and (nrow * h // steps) % BF16_ROWS == 0)
        slab = nrow * hold // steps
        index = lambda *g, hold=hold: step_of(*g) // hold
        in_specs.append(pl.BlockSpec((None, slab, ncol),
                                     lambda *g, layer=layer, index=index: (layer, index(*g), 0)))
        if width is None:
            out_shape.append(jax.ShapeDtypeStruct((nrow, ncol), _BF16))
            out_specs.append(pl.BlockSpec((slab, ncol), lambda *g, index=index: (index(*g), 0)))
        else:
            out_shape.append(jax.ShapeDtypeStruct((ncol // width, nrow, width), _BF16))
            out_specs.append(pl.BlockSpec((ncol // width, slab, width),
                                          lambda *g, index=index: (0, index(*g), 0)))
    return in_specs, out_specs, out_shape


def _tril_kernel(w_ref, o_ref):
    tri = (lax.broadcasted_iota(jnp.int32, w_ref.shape, 1)
           >= lax.broadcasted_iota(jnp.int32, w_ref.shape, 2))
    o_ref[...] = jnp.where(tri, w_ref[...], 0.0).astype(_BF16)


def _tril(w_spatial):
    depth = w_spatial.shape[0]
    block = (None,) + tuple(w_spatial.shape[1:])
    return pl.pallas_call(
        _tril_kernel, grid=(depth,),
        in_specs=[pl.BlockSpec(block, lambda l: (l, 0, 0, 0))],
        out_specs=pl.BlockSpec(block, lambda l: (l, 0, 0, 0)),
        out_shape=jax.ShapeDtypeStruct(w_spatial.shape, _BF16),
        name="tril_spatial",
    )(w_spatial)


def _convert_kernel(*refs):
    half = len(refs) // 2
    for src, dst in zip(refs[:half], refs[half:]):
        _convert_slab(src, dst)


def _convert(casts, steps, name):
    in_specs, out_specs, out_shape = _convert_specs(casts, steps, lambda i: i)
    return pl.pallas_call(
        _convert_kernel, grid=(steps,), in_specs=in_specs, out_specs=out_specs, out_shape=out_shape,
        compiler_params=pltpu.CompilerParams(
            dimension_semantics=("arbitrary",), vmem_limit_bytes=VMEM_LIMIT),
        name=name,
    )(*[w for w, _, _ in casts])


def _mixer_kernel(*refs, seg, nseg, carry, emit_vrows, ncast):
    it = iter(refs)
    take = lambda n: [next(it) for _ in range(n)]
    x_ref = take(1)[0]
    conv0_ref, h0_ref = (None, None) if carry else take(2)
    norm1_ref, cvec_ref = take(2)
    win_ref, wgate_ref, wsp_ref, bsb_ref, wout_ref = take(5)
    cast_src = take(ncast)
    y_ref, convo_ref, ho_ref = take(3)
    vrows_ref = take(1)[0] if emit_vrows else None
    cast_dst = take(ncast)
    xpad, a3, b3, h3, hcar = take(5)
    rows = seg * nseg
    gw = GROUP_WIDTH
    groups_per_seg = seg // SUBLANES

    def vec(k, cols=slice(None)):
        return cvec_ref[k:k + 1, cols]

    def pair_cols(p):
        return slice(p * PAIR, (p + 1) * PAIR)

    state_rows = slice(CARRY_ROWS - CONV_STATE, CARRY_ROWS)
    if carry:
        t = pl.program_id(1)

        @pl.when(t == 0)
        def _():
            xpad[0, state_rows, :] = jnp.zeros((CONV_STATE, gw), _F32)
            hcar[...] = jnp.zeros_like(hcar)

        @pl.when(t > 0)
        def _():
            xpad[0, state_rows, :] = xpad[0, seg + CARRY_ROWS - CONV_STATE:seg + CARRY_ROWS, :]
    else:
        for s in range(nseg):
            xpad[s, state_rows, :] = conv0_ref[s]

    x = x_ref[...]
    xnb = _rms(x, norm1_ref[...]).astype(_BF16)

    def proj(group, p):
        return _dot(xnb, win_ref[group * NPAIR + p])

    def conv(p, xa_p):
        cols = pair_cols(p)
        for s in range(nseg):
            xpad[s, CARRY_ROWS:CARRY_ROWS + seg, cols] = xa_p[s * seg:(s + 1) * seg]
            convo_ref[s, :, cols] = xa_p[(s + 1) * seg - CONV_STATE:(s + 1) * seg]

        def shifted(k):
            parts = [xpad[s, CARRY_ROWS - k:CARRY_ROWS - k + seg, cols] for s in range(nseg)]
            return parts[0] if nseg == 1 else jnp.concatenate(parts, axis=0)

        return (shifted(3) * vec(_CW0, cols) + shifted(2) * vec(_CW0 + 1, cols)
                + shifted(1) * vec(_CW0 + 2, cols) + xa_p * vec(_CW0 + 3, cols) + vec(_CB, cols))

    def gates(p, xc_p):
        g = [_dot(xc_p[:, hh * HEAD_DIM:(hh + 1) * HEAD_DIM].astype(_BF16), wgate_ref[CHUNK_HEADS * p + hh])
             for hh in range(CHUNK_HEADS)]
        return (jnp.concatenate([gh[:, :HEAD_DIM] for gh in g], axis=1),
                jnp.concatenate([gh[:, HEAD_DIM:] for gh in g], axis=1))

    neg_lam = -vec(_LAM)
    softplus = jnp.maximum(neg_lam, 0.0) + jnp.log1p(jnp.exp(-jnp.abs(neg_lam)))
    neg_c_softplus = -LRU_C * softplus

    def lru_coeffs(p, xc_p, g_r, g_i):
        cols = pair_cols(p)
        for half in range(2):
            rsl = slice(half * rows // 2, (half + 1) * rows // 2)
            gsl = slice(half * rows // 2 // SUBLANES, (half + 1) * rows // 2 // SUBLANES)
            r = _sigmoid(g_r[rsl] + vec(_BR, cols))
            ig = _sigmoid(g_i[rsl] + vec(_BI, cols))
            log_a = r * neg_c_softplus[:, cols]
            a = jnp.exp(log_a)
            m = -jnp.tanh(log_a) * (a * a + 1.0)
            bterm = (m * lax.rsqrt(jnp.maximum(m, _MIN_NORMAL))) * (ig * xc_p[rsl])
            for hh in range(CHUNK_HEADS):
                c = CHUNK_HEADS * p + hh
                sub = slice(hh * HEAD_DIM, (hh + 1) * HEAD_DIM)
                shape = (rows // 2 // SUBLANES, SUBLANES, HEAD_DIM)
                a3[gsl, c * SUBLANES:(c + 1) * SUBLANES, :] = a[:, sub].reshape(shape)
                b3[gsl, c * SUBLANES:(c + 1) * SUBLANES, :] = bterm[:, sub].reshape(shape)

    def scan_part(q, hs):
        per = groups_per_seg // SCAN_PARTS
        hs = list(hs)
        for j in range(q * per, (q + 1) * per):
            for rr in range(SUBLANES):
                for s in range(nseg):
                    jj = s * groups_per_seg + j
                    at = a3[jj, pl.ds(rr, HEADS, stride=SUBLANES), :]
                    bt = b3[jj, pl.ds(rr, HEADS, stride=SUBLANES), :]
                    hs[s] = at * hs[s] + bt
                    h3[jj, pl.ds(rr, HEADS, stride=SUBLANES), :] = hs[s]
        return hs

    def y_lru_pair(p):
        return jnp.concatenate(
            [h3[:, c * SUBLANES:(c + 1) * SUBLANES, :].reshape(rows, HEAD_DIM)
             for c in range(CHUNK_HEADS * p, CHUNK_HEADS * (p + 1))],
            axis=1)

    def layernorm_v(vg):
        mu = _rowsum(vg) * (1.0 / gw)
        vc = [g - mu for g in vg]
        rstd = lax.rsqrt(_rowsum([c * c for c in vc]) * (1.0 / gw) + EPS)
        return [vc[p] * rstd * vec(_LNG, pair_cols(p)) + vec(_LNB, pair_cols(p)) for p in range(NPAIR)]

    chunk = min(seg, MLP_CHUNK)
    nchunk = rows // chunk

    def token_mlp(h, vb, gu):
        p, hh = divmod(h, CHUNK_HEADS)
        sub = slice(hh * HEAD_DIM, (hh + 1) * HEAD_DIM)
        vh = jnp.concatenate([vb[p][k * chunk:(k + 1) * chunk, sub] for k in range(nchunk)], axis=1)
        m = _dot(wsp_ref[h, 0:chunk, 0:chunk], vh)
        bias = bsb_ref[h][:chunk]
        mixed = jnp.concatenate(
            [m[:, k * HEAD_DIM:(k + 1) * HEAD_DIM] + bias for k in range(nchunk)], axis=0)
        return gu[p][:, sub] * mixed

    def convert_weights():
        for src, dst in zip(cast_src, cast_dst):
            _convert_slab(src, dst)

    xa0 = proj(0, 0)
    xa1 = proj(0, 1)
    xc0 = conv(0, xa0)
    xa2 = proj(0, 2)
    xc1 = conv(1, xa1)
    gt0 = gates(0, xc0)
    xa3 = proj(0, 3)
    xc2 = conv(2, xa2)
    gt1 = gates(1, xc1)
    v = [None] * NPAIR
    v[0] = proj(3, 0)
    xc3 = conv(3, xa3)
    gt2 = gates(2, xc2)
    v[1] = proj(3, 1)
    lru_coeffs(0, xc0, *gt0)
    gt3 = gates(3, xc3)
    v[2] = proj(3, 2)
    lru_coeffs(1, xc1, *gt1)
    v[3] = proj(3, 3)
    lru_coeffs(2, xc2, *gt2)
    u = [None] * NPAIR
    u[0] = proj(2, 0)
    lru_coeffs(3, xc3, *gt3)

    hs = [hcar[...] if carry else h0_ref[s] for s in range(nseg)]
    u[1] = proj(2, 1)
    hs = scan_part(0, hs)
    vg = [_gelu(v[0]), _gelu(v[1])]
    u[2] = proj(2, 2)
    convert_weights()
    hs = scan_part(1, hs)
    vg += [_gelu(v[2]), _gelu(v[3])]
    u[3] = proj(2, 3)
    hs = scan_part(2, hs)
    v_n = layernorm_v(vg)
    if emit_vrows:
        for p in range(NPAIR):
            vrows_ref[:, pair_cols(p)] = v_n[p]
    vb = [n.astype(_BF16) for n in v_n]
    ga = [None] * NPAIR
    ga[0] = proj(1, 0)
    hs = scan_part(3, hs)
    if carry:
        hcar[...] = hs[0]
    for s in range(nseg):
        ho_ref[s] = hs[s]
    gu = [_gelu(u[p]) for p in range(NPAIR)]
    ga[1] = proj(1, 1)
    out_b = [token_mlp(h, vb, gu) for h in range(HEADS // 2)]
    ga[2] = proj(1, 2)
    out_b += [token_mlp(h, vb, gu) for h in range(HEADS // 2, HEADS)]
    gga = [_gelu(ga[0]), _gelu(ga[1])]
    ga[3] = proj(1, 3)
    rs_b = lax.rsqrt(_rowsum([o * o for o in out_b]) * (1.0 / gw) + EPS)
    nb = jnp.concatenate(
        [out_b[h] * rs_b * vec(_GNB, slice(h * HEAD_DIM, (h + 1) * HEAD_DIM)) for h in range(HEADS)],
        axis=1).astype(_BF16)
    gga.append(_gelu(ga[2]))

    acc = [_dot(nb, wout_ref[n, gw:2 * gw, :]) for n in range(OUT_CHUNKS)]
    gga.append(_gelu(ga[3]))
    out_a = [y_lru_pair(p) * gga[p] for p in range(NPAIR)]
    rs_a = lax.rsqrt(_rowsum([o * o for o in out_a]) * (1.0 / gw) + EPS)
    na = jnp.concatenate([out_a[p] * rs_a * vec(_GNA, pair_cols(p)) for p in range(NPAIR)],
                         axis=1).astype(_BF16)
    for n in range(OUT_CHUNKS):
        cols = slice(n * OUT_COLS, (n + 1) * OUT_COLS)
        y_ref[:, cols] = x[:, cols] + acc[n] + _dot(na, wout_ref[n, 0:gw, :])


def _layer_spec(shape, layer):
    zeros = (0,) * (len(shape) - 1)
    return pl.BlockSpec((None,) + tuple(shape[1:]), lambda *_: (layer,) + zeros,
                        pipeline_mode=pl.Buffered(1))


def _whole_spec(shape):
    zeros = (0,) * len(shape)
    return pl.BlockSpec(tuple(shape), lambda *_: zeros, pipeline_mode=pl.Buffered(1))


def _mixer(x2d, state, norm1, cvec, win, wgate, wsp, bsb, wout, casts=(), *, layer, nseq, seg, nseg,
           carry, emit_vrows, name):
    rows = seg * nseg
    total = x2d.shape[0]
    gw = GROUP_WIDTH
    if carry:
        steps = total // nseq // rows
        grid = (nseq, steps)
        row_map = lambda b, t: (b * steps + t, 0)
        seq_map = lambda b, t: (b, 0, 0)
        state_specs = []
        cast_in, cast_out, cast_shape = _convert_specs(casts, nseq * steps, lambda b, t: b * steps + t)
    else:
        assert not casts
        grid = (total // rows,)
        row_map = lambda i: (i, 0)
        seq_map = lambda i: (i, 0, 0)
        state_map = lambda i: (layer, i, 0, 0)
        state_specs = [pl.BlockSpec((None, nseg, CONV_STATE, gw), state_map),
                       pl.BlockSpec((None, nseg, HEADS, HEAD_DIM), state_map)]
        cast_in, cast_out, cast_shape = [], [], []
    out_shape = [jax.ShapeDtypeStruct((total, D_MODEL), _F32),
                 jax.ShapeDtypeStruct((nseq, CONV_STATE, gw), _F32),
                 jax.ShapeDtypeStruct((nseq, HEADS, HEAD_DIM), _F32)]
    out_specs = [pl.BlockSpec((rows, D_MODEL), row_map),
                 pl.BlockSpec((nseg, CONV_STATE, gw), seq_map),
                 pl.BlockSpec((nseg, HEADS, HEAD_DIM), seq_map)]
    if emit_vrows:
        out_shape.append(jax.ShapeDtypeStruct((total, gw), _F32))
        out_specs.append(pl.BlockSpec((rows, gw), row_map))
    scan_shape = (rows // SUBLANES, HEADS * SUBLANES, HEAD_DIM)
    return pl.pallas_call(
        functools.partial(_mixer_kernel, seg=seg, nseg=nseg, carry=carry, emit_vrows=emit_vrows,
                          ncast=len(casts)),
        grid=grid,
        in_specs=[pl.BlockSpec((rows, D_MODEL), row_map), *state_specs,
                  _layer_spec(norm1.shape, layer), _layer_spec(cvec.shape, layer),
                  _whole_spec(win.shape), _layer_spec(wgate.shape, layer),
                  _layer_spec(wsp.shape, layer), _layer_spec(bsb.shape, layer),
                  _whole_spec(wout.shape), *cast_in],
        out_specs=out_specs + cast_out,
        out_shape=out_shape + cast_shape,
        scratch_shapes=[pltpu.VMEM((nseg, seg + CARRY_ROWS, gw), _F32),
                        pltpu.VMEM(scan_shape, _F32), pltpu.VMEM(scan_shape, _F32),
                        pltpu.VMEM(scan_shape, _F32), pltpu.VMEM((HEADS, HEAD_DIM), _F32)],
        compiler_params=pltpu.CompilerParams(
            dimension_semantics=("arbitrary",) * len(grid), vmem_limit_bytes=VMEM_LIMIT),
        name=name,
    )(x2d, *state, norm1, cvec, win, wgate, wsp, bsb, wout, *[w for w, _, _ in casts])


def _ffn_kernel(h_ref, norm2_ref, wg_ref, wu_ref, *rest, final):
    wd_refs, (normf_ref, o_ref, hn_ref) = rest[:OUT_CHUNKS], rest[OUT_CHUNKS:]
    j = pl.program_id(1)
    last = pl.num_programs(1) - 1

    def add_slab(base_ref, normalise=False):
        hn = hn_ref[...]
        g = _dot(hn, wg_ref[...])
        u = _dot(hn, wu_ref[...])
        act = ((g * _sigmoid(g)) * u).astype(_BF16)
        chunks = [slice(n * OUT_COLS, (n + 1) * OUT_COLS) for n in range(OUT_CHUNKS)]
        outs = [base_ref[:, cols] + _dot(act, wd_refs[n][...]) for n, cols in enumerate(chunks)]
        if normalise:
            rs = lax.rsqrt(_rowsum([o * o for o in outs]) * (1.0 / D_MODEL) + EPS)
            outs = [o * rs * normf_ref[:, cols] for o, cols in zip(outs, chunks)]
        for o, cols in zip(outs, chunks):
            o_ref[:, cols] = o

    @pl.when(j == 0)
    def _():
        hn_ref[...] = _rms(h_ref[...], norm2_ref[...]).astype(_BF16)
        add_slab(h_ref)

    @pl.when((j > 0) & (j < last) if final else (j > 0))
    def _():
        add_slab(o_ref)

    if final:
        @pl.when(j == last)
        def _():
            add_slab(o_ref, normalise=True)


def _ffn(h2d, norm2, wg, wu, wd, normf, *, layer, final, name):
    total = h2d.shape[0]
    d_ff = wg.shape[-1]
    rows = min(FFN_ROWS, total)
    grid = (total // rows, d_ff // FFN_COLS)
    assert grid[1] >= 2, "the first and the last slab of a row tile must be different grid steps"
    return pl.pallas_call(
        functools.partial(_ffn_kernel, final=final),
        grid=grid,
        in_specs=[pl.BlockSpec((rows, D_MODEL), lambda i, j: (i, 0)),
                  _layer_spec(norm2.shape, layer),
                  pl.BlockSpec((D_MODEL, FFN_COLS), lambda i, j: (0, j)),
                  pl.BlockSpec((D_MODEL, FFN_COLS), lambda i, j: (0, j)),
                  *[pl.BlockSpec((FFN_COLS, OUT_COLS), lambda i, j, n=n: (j, n)) for n in range(OUT_CHUNKS)],
                  _layer_spec(normf.shape, 0)],
        out_specs=pl.BlockSpec((rows, D_MODEL), lambda i, j: (i, 0)),
        out_shape=jax.ShapeDtypeStruct((total, D_MODEL), _F32),
        scratch_shapes=[pltpu.VMEM((rows, D_MODEL), _BF16)],
        compiler_params=pltpu.CompilerParams(
            dimension_semantics=("arbitrary", "arbitrary"), vmem_limit_bytes=VMEM_LIMIT),
        name=name,
    )(h2d, norm2, wg, wu, *([wd] * OUT_CHUNKS), normf)


def kernel(x_prompt, x_sample, state_conv, state_lru, norm1, w_in, conv_w, conv_b, w_rgate, b_rgate,
           w_igate, b_igate, lru_param, v_ln_g, v_ln_b, w_spatial, b_spatial, gn_a, gn_b, w_out,
           norm2, w_gate, w_up, w_down, norm_f):
    depth = w_in.shape[0]
    batch, seq, _ = x_prompt.shape
    dec_batch, dec_seq, _ = x_sample.shape
    gw = GROUP_WIDTH

    xp = x_prompt.reshape(batch * seq, D_MODEL)
    xs = x_sample.reshape(dec_batch * dec_seq, D_MODEL)
    sample_state = (state_conv, state_lru.reshape(depth, dec_batch, HEADS, HEAD_DIM))

    row = lambda a: a[:, None, :]
    cvec = jnp.concatenate(
        [conv_w, row(conv_b), row(b_rgate), row(b_igate), row(lru_param), row(v_ln_g), row(v_ln_b),
         row(gn_a), row(gn_b), jnp.zeros((depth, _CVEC_ROWS - 12, gw), _F32)], axis=1)
    wgate = jnp.concatenate([w_rgate, w_igate], axis=-1).astype(_BF16)
    bsb = jnp.broadcast_to(b_spatial[..., None], (depth, HEADS, MLP_CHUNK, HEAD_DIM))
    normf = norm_f.reshape(1, 1, D_MODEL)
    steps = seq // MIXER_ROWS

    def proj_casts(l):
        return [(w_in, l, PAIR), (w_out, l, OUT_COLS)]

    win, wout = _convert(proj_casts(0), CONVERT_STEPS, name="convert_proj_0")
    wsp = _tril(w_spatial)

    conv_p, lru_p, conv_s, lru_s, vrows_s = [], [], [], [], []
    for l in range(depth):
        final = l == depth - 1
        mixer_w = (norm1[:, None, :], cvec, win, wgate, wsp, bsb, wout)
        casts = [(w_gate, l, None), (w_up, l, None), (w_down, l, None)]
        if not final:
            casts += proj_casts(l + 1)
        hp, cp, lp, wg, wu, wd, *nxt = _mixer(xp, (), *mixer_w, casts, layer=l, nseq=batch,
                                              seg=MIXER_ROWS, nseg=1, carry=True, emit_vrows=False,
                                              name=f"mixer_prompt_{l}")
        ffn_w = (norm2[:, None, :], wg, wu, wd, normf)
        hs, cs, ls, vs = _mixer(xs, sample_state, *mixer_w, layer=l, nseq=dec_batch, seg=dec_seq,
                                nseg=MIXER_ROWS // dec_seq, carry=False, emit_vrows=True,
                                name=f"mixer_sample_{l}")
        xp = _ffn(hp, *ffn_w, layer=l, final=final, name=f"ffn_prompt_{l}")
        xs = _ffn(hs, *ffn_w, layer=l, final=final, name=f"ffn_sample_{l}")
        if not final:
            win, wout = nxt

        conv_p.append(cp)
        lru_p.append(lp.reshape(batch, gw))
        conv_s.append(cs)
        lru_s.append(ls.reshape(dec_batch, gw))
        vrows_s.append(vs.reshape(dec_batch, dec_seq, gw))

    return (xp.reshape(batch, seq, D_MODEL), xs.reshape(dec_batch, dec_seq, D_MODEL),
            jnp.stack(conv_p), jnp.stack(lru_p), jnp.stack(conv_s), jnp.stack(lru_s),
            jnp.stack(vrows_s))
```
